```python
import math
import jax, jax.numpy as jnp
from jax import lax
import numpy as np

D_MODEL = 2048
BATCH = 2
SEQ = 16384
DEPTH = 2
DEC_BATCH = 16
DEC_SEQ = 32
PAST_LEN = 1024

CHUNK = 64
D_MIX = 2048
RET_HEADS = 4
RET_DK = 64
RET_DV = 128
LRU_WIDTH = 512
LRU_BLOCKS = 4
LRU_CONV = 4
LRU_C = 8.0
SSD_INNER = 512
SSD_HEADDIM = 64
SSD_HEADS = SSD_INNER // SSD_HEADDIM
SSD_GROUPS = 2
SSD_STATE = 128
SSD_CONV = 4
RWKV_WIDTH = 512
RWKV_HEADDIM = 64
RWKV_HEADS = RWKV_WIDTH // RWKV_HEADDIM
RWKV_DECAY_LORA = 64
RWKV_A_LORA = 64
RWKV_GATE_LORA = 128
RWKV_LN_EPS = 64e-5
N_EXPERTS = 64
TOP_K = 8
N_EXPERT_GROUPS = 8
TOPK_GROUPS = 4
D_EXPERT = 512
D_SHARED = 512
ROUTED_SCALE = 2.5
MOE_BLOCK = 256
DN_ALPHA = (2 * DEPTH) ** 0.25
DN_BETA = (8 * DEPTH) ** -0.25
LN_EPS = 1e-5
RMS_EPS = 1e-6

A_QK = RET_HEADS * RET_DK
A_V = RET_HEADS * RET_DV
COLS_A = 2 * A_QK + 2 * A_V
COLS_B = 2 * LRU_WIDTH
SSD_CONV_DIM = SSD_INNER + 2 * SSD_GROUPS * SSD_STATE
COLS_C = SSD_INNER + SSD_CONV_DIM + SSD_HEADS
COLS_D = 3 * RWKV_WIDTH + RWKV_DECAY_LORA + RWKV_A_LORA + RWKV_GATE_LORA
D_IN = COLS_A + COLS_B + COLS_C + COLS_D

kernel_name = 'hybrid_streaming_encoder_step'

F32 = jnp.float32


def _split(t, sizes):
    return jnp.split(t, np.cumsum(sizes)[:-1].tolist(), axis=-1)


def _rms(x, eps=RMS_EPS):
    xf = x.astype(F32)
    return xf * lax.rsqrt(jnp.mean(xf * xf, axis=-1, keepdims=True) + eps)


def _ln(x, eps):
    xf = x.astype(F32)
    xc = xf - jnp.mean(xf, axis=-1, keepdims=True)
    return xc * lax.rsqrt(jnp.mean(xc * xc, axis=-1, keepdims=True) + eps)


def _layer_norm(x, g, b):
    return _ln(x, LN_EPS).astype(x.dtype) * g + b


def _rotary(x, pos):
    half = x.shape[-1] // 2
    freq = 10000.0 ** (-jnp.linspace(0.0, 1.0, half, dtype=F32))
    ang = pos.astype(F32)[:, None] * freq
    cos = jnp.cos(ang)[None, :, None, :].astype(x.dtype)
    sin = jnp.sin(ang)[None, :, None, :].astype(x.dtype)
    x1, x2 = x[..., :half], x[..., half:]
    return jnp.concatenate([x1 * cos - x2 * sin, x1 * sin + x2 * cos], axis=-1)


def _causal_conv(x, buf, w, b):
    k_w, seq = w.shape[0], x.shape[1]
    xp = jnp.concatenate([buf.astype(x.dtype), x], axis=1)
    y = b + xp[:, 0:seq] * w[0]
    for j in range(1, k_w):
        y = y + xp[:, j:j + seq] * w[j]
    return y, xp[:, seq:]


def _to_chunks(t, cs):
    b, l = t.shape[:2]
    return jnp.swapaxes(t.reshape((b, l // cs, cs) + t.shape[2:]), 0, 1)


def _from_chunks(t):
    n, b, cs = t.shape[:3]
    return jnp.swapaxes(t, 0, 1).reshape((b, n * cs) + t.shape[3:])


def _retention(q, k, v, s0):
    seq = q.shape[1]
    cs = math.gcd(seq, CHUNK)
    lg = jnp.log1p(-(2.0 ** (-5.0 - jnp.arange(RET_HEADS, dtype=F32))))
    idx = jnp.arange(cs, dtype=F32)
    rel = idx[:, None] - idx[None, :]
    causal = rel >= 0
    dmask = jnp.where(causal[None], jnp.exp(jnp.where(causal, rel, 0.0)[None] * lg[:, None, None]), 0.0)
    q_dec = jnp.exp((idx[:, None] + 1.0) * lg[None, :])[None, :, :, None]
    k_dec = jnp.exp((cs - 1.0 - idx[:, None]) * lg[None, :])[None, :, :, None]
    c_dec = jnp.exp(cs * lg)[None, :, None, None]

    def step(s, inp):
        qc, kc, vc = inp
        att = jnp.einsum('bthd,bshd->bhts', qc, kc) * dmask
        o = jnp.einsum('bhts,bshe->bthe', att, vc) + jnp.einsum('bthd,bhde->bthe', qc, s) * q_dec
        s = s * c_dec + jnp.einsum('bshd,bshe->bhde', kc * k_dec, vc)
        return s, o

    xs = tuple(_to_chunks(t.astype(F32), cs) for t in (q, k, v))
    s, o = lax.scan(step, s0.astype(F32), xs)
    return _from_chunks(o), s


def _lin_comb(left, right):
    a_l, b_l = left
    a_r, b_r = right
    return a_l * a_r, a_r * b_l + b_r


def _rg_lru(xb, h0, wa, ba, wx, bx, lam, pos):
    bsz, seq, width = xb.shape
    xf = xb.astype(F32)
    xh = xf.reshape(bsz, seq, LRU_BLOCKS, width // LRU_BLOCKS)
    r = jax.nn.sigmoid(jnp.einsum('blnc,ncd->blnd', xh, wa.astype(F32)).reshape(bsz, seq, width) + ba.astype(F32))
    i = jax.nn.sigmoid(jnp.einsum('blnc,ncd->blnd', xh, wx.astype(F32)).reshape(bsz, seq, width) + bx.astype(F32))
    log_a = -LRU_C * r * jax.nn.softplus(-lam.astype(F32))
    a = jnp.exp(log_a)
    mult = jnp.where((pos == 0)[None, :, None], 1.0, jnp.sqrt(-jnp.expm1(2.0 * log_a)))
    b = mult * (i * xf)
    b = b.at[:, 0].add(a[:, 0] * h0.astype(F32))
    _, h = lax.associative_scan(_lin_comb, (a, b), axis=1)
    return h, h[:, -1]


def _ssd(x, dt, a, bm, cm, s0):
    seq = x.shape[1]
    cs = math.gcd(seq, CHUNK)
    rep = SSD_HEADS // SSD_GROUPS
    tri = jnp.tril(jnp.ones((cs, cs), dtype=bool))[None, :, :, None]

    def step(s, inp):
        xc, dtc, bc, cc = inp
        cum = jnp.cumsum(dtc * a, axis=1)
        seg = cum[:, :, None, :] - cum[:, None, :, :]
        lmat = jnp.exp(jnp.where(tri, seg, -jnp.inf))
        cb = jnp.repeat(jnp.einsum('btgn,bsgn->btsg', cc, bc), rep, axis=-1)
        y = jnp.einsum('btsh,bsh,bshp->bthp', cb * lmat, dtc, xc)
        ch = jnp.repeat(cc, rep, axis=2)
        bh = jnp.repeat(bc, rep, axis=2)
        y = y + jnp.einsum('bthn,bhpn->bthp', ch, s) * jnp.exp(cum)[..., None]
        to_end = jnp.exp(cum[:, -1:, :] - cum) * dtc
        s = s * jnp.exp(cum[:, -1, :])[:, :, None, None] + jnp.einsum('bsh,bshn,bshp->bhpn', to_end, bh, xc)
        return s, y

    xs = tuple(_to_chunks(t.astype(F32), cs) for t in (x, dt, bm, cm))
    s, y = lax.scan(step, s0.astype(F32), xs)
    return _from_chunks(y), s


def _rwkv7(r, w, k, v, kk, a, s0):
    def step(s, inp):
        rt, wt, kt, vt, kkt, at = inp
        sa = jnp.einsum('bhvk,bhk->bhv', s, -kkt)
        s = s * wt[:, :, None, :] + sa[..., None] * (kkt * at)[:, :, None, :] + vt[..., None] * kt[:, :, None, :]
        return s, jnp.einsum('bhvk,bhk->bhv', s, rt)

    xs = tuple(jnp.swapaxes(t.astype(F32), 0, 1) for t in (r, w, k, v, kk, a))
    s, o = lax.scan(step, s0.astype(F32), xs)
    return jnp.swapaxes(o, 0, 1), s


def _moe(x, router_w, router_bias, e_gate, e_up, e_down, s_gate, s_up, s_down):
    bsz, seq, dm = x.shape
    xt = x.reshape(-1, dm)
    n_tok = xt.shape[0]
    scores = jax.nn.sigmoid((xt @ router_w).astype(F32))
    biased = scores + router_bias.astype(F32)
    grp_top2 = lax.top_k(biased.reshape(n_tok, N_EXPERT_GROUPS, -1), 2)[0].sum(-1)
    _, top_grp = lax.top_k(grp_top2, TOPK_GROUPS)
    grp_keep = jnp.any(top_grp[:, :, None] == jnp.arange(N_EXPERT_GROUPS)[None, None, :], axis=1)
    keep = jnp.repeat(grp_keep, N_EXPERTS // N_EXPERT_GROUPS, axis=1)
    _, idx = lax.top_k(jnp.where(keep, biased, -jnp.inf), TOP_K)
    gate = jnp.take_along_axis(scores, idx, axis=1)
    gate = gate / jnp.sum(gate, axis=-1, keepdims=True) * ROUTED_SCALE
    n_as = n_tok * TOP_K
    flat_e = idx.reshape(-1)
    flat_t = jnp.repeat(jnp.arange(n_tok, dtype=jnp.int32), TOP_K)
    flat_g = gate.reshape(-1)
    counts = jnp.bincount(flat_e, length=N_EXPERTS)
    padded = (counts + MOE_BLOCK - 1) // MOE_BLOCK * MOE_BLOCK
    pad_end = jnp.cumsum(padded)
    pad_start = pad_end - padded
    start = jnp.cumsum(counts) - counts
    order = jnp.argsort(flat_e)
    se = flat_e[order]
    dest = pad_start[se] + jnp.arange(n_as) - start[se]
    n_blocks = -(-n_as // MOE_BLOCK) + N_EXPERTS
    n_rows = n_blocks * MOE_BLOCK
    row_tok = jnp.full((n_rows,), n_tok, jnp.int32).at[dest].set(flat_t[order])
    row_g = jnp.zeros((n_rows,), x.dtype).at[dest].set(flat_g[order].astype(x.dtype))
    block_e = jnp.minimum(jnp.searchsorted(pad_end, jnp.arange(n_blocks) * MOE_BLOCK, side='right'), N_EXPERTS - 1)
    x_pad = jnp.concatenate([xt, jnp.zeros((1, dm), xt.dtype)], axis=0)

    def block(acc, inp):
        toks, g, e = inp
        xb = x_pad[toks]
        h = jax.nn.silu(xb @ e_gate[e]) * (xb @ e_up[e])
        return acc.at[toks].add((h @ e_down[e]) * g[:, None]), None

    acc, _ = lax.scan(block, jnp.zeros((n_tok + 1, dm), x.dtype),
                      (row_tok.reshape(n_blocks, MOE_BLOCK), row_g.reshape(n_blocks, MOE_BLOCK), block_e))
    shared = (jax.nn.silu(xt @ s_gate) * (xt @ s_up)) @ s_down
    return (acc[:n_tok] + shared).reshape(bsz, seq, dm)


def _layer(x, pos, st, p):
    bsz, seq, _ = x.shape
    dt_ = x.dtype
    proj = x @ p['w_in']
    pa, pb, pc, pd = _split(proj, [COLS_A, COLS_B, COLS_C, COLS_D])

    q, k, v, g = _split(pa, [A_QK, A_QK, A_V, A_V])
    q = _rotary(q.reshape(bsz, seq, RET_HEADS, RET_DK), pos)
    k = _rotary(k.reshape(bsz, seq, RET_HEADS, RET_DK), pos) * (RET_DK ** -0.5)
    o, ret_s = _retention(q, k, v.reshape(bsz, seq, RET_HEADS, RET_DV), st['ret'])
    o_a = _rms(o).reshape(bsz, seq, A_V).astype(dt_) * p['ret_norm_w'] * jax.nn.silu(g)

    xb, gb = _split(pb, [LRU_WIDTH, LRU_WIDTH])
    xb, lru_conv = _causal_conv(xb, st['lru_conv'], p['lru_conv_w'], p['lru_conv_b'])
    h, lru_h = _rg_lru(xb, st['lru_h'], p['lru_wa'], p['lru_ba'], p['lru_wx'], p['lru_bx'], p['lru_lambda'], pos)
    o_b = _rms(h * jax.nn.gelu(gb.astype(F32))).astype(dt_) * p['lru_norm_w']

    z, xbc, dtr = _split(pc, [SSD_INNER, SSD_CONV_DIM, SSD_HEADS])
    xbc, ssd_conv = _causal_conv(xbc, st['ssd_conv'], p['ssd_conv_w'], p['ssd_conv_b'])
    xs, bm, cm = _split(jax.nn.silu(xbc), [SSD_INNER, SSD_GROUPS * SSD_STATE, SSD_GROUPS * SSD_STATE])
    dt = jax.nn.softplus(dtr.astype(F32) + p['ssd_dt_bias'].astype(F32))
    a_ssd = -jnp.exp(p['ssd_A_log'].astype(F32))
    xs = xs.reshape(bsz, seq, SSD_HEADS, SSD_HEADDIM)
    y, ssd_s = _ssd(xs, dt, a_ssd, bm.reshape(bsz, seq, SSD_GROUPS, SSD_STATE),
                    cm.reshape(bsz, seq, SSD_GROUPS, SSD_STATE), st['ssd'])
    y = (y + xs.astype(F32) * p['ssd_D'].astype(F32)[:, None]).reshape(bsz, seq, SSD_INNER) * jax.nn.silu(z.astype(F32))
    o_c = _rms(y.reshape(bsz, seq, SSD_GROUPS, -1)).reshape(bsz, seq, SSD_INNER).astype(dt_) * p['ssd_norm_w']

    prev = jnp.concatenate([st['rwkv_shift'].astype(dt_), pd[:, :-1]], axis=1)
    rwkv_shift = pd[:, -1:]
    mixed = pd + (prev - pd) * p['rwkv_mu']
    r, kd, vd, wd, ad, gd = _split(mixed, [RWKV_WIDTH, RWKV_WIDTH, RWKV_WIDTH, RWKV_DECAY_LORA, RWKV_A_LORA, RWKV_GATE_LORA])
    heads = lambda t: t.reshape(bsz, seq, RWKV_HEADS, RWKV_HEADDIM)
    w_log = -jax.nn.softplus(-(p['rwkv_w0'] + jnp.tanh(wd) @ p['rwkv_w2']).astype(F32)) - 0.5
    decay = jnp.exp(-jnp.exp(w_log))
    iclr = jax.nn.sigmoid((p['rwkv_a0'] + ad @ p['rwkv_a2']).astype(F32))
    out_gate = jax.nn.sigmoid(gd) @ p['rwkv_g2']
    kk = heads((kd * p['rwkv_k_k']).astype(F32))
    kk = kk / jnp.maximum(jnp.sqrt(jnp.sum(kk * kk, axis=-1, keepdims=True)), 1e-12)
    kr = heads(kd.astype(F32) * (1.0 + (iclr - 1.0) * p['rwkv_k_a'].astype(F32)))
    rh = heads(r.astype(F32))
    vh = heads(vd.astype(F32))
    o, rwkv_s = _rwkv7(rh, heads(decay), kr, vh, kk, heads(iclr), st['rwkv'])
    o = _ln(o, RWKV_LN_EPS).reshape(bsz, seq, RWKV_WIDTH) * p['rwkv_ln_w'].astype(F32) + p['rwkv_ln_b'].astype(F32)
    bonus = jnp.sum(rh * kr * p['rwkv_r_k'].astype(F32), axis=-1, keepdims=True) * vh
    o_d = (o + bonus.reshape(bsz, seq, RWKV_WIDTH)).astype(dt_) * out_gate

    mix = jnp.concatenate([o_a, o_b, o_c, o_d], axis=-1) @ p['w_out']
    x = _layer_norm(DN_ALPHA * x + mix, p['ln1_g'], p['ln1_b'])
    ff = _moe(x, p['router_w'], p['router_bias'], p['exp_w_gate'], p['exp_w_up'], p['exp_w_down'],
              p['sh_w_gate'], p['sh_w_up'], p['sh_w_down'])
    x = _layer_norm(DN_ALPHA * x + ff, p['ln2_g'], p['ln2_b'])
    new = {'ret': ret_s, 'lru_h': lru_h, 'lru_conv': lru_conv, 'ssd': ssd_s,
           'ssd_conv': ssd_conv, 'rwkv': rwkv_s, 'rwkv_shift': rwkv_shift}
    return x, new


def setup_inputs(seed: int = 0) -> dict:
    key = jax.random.key(seed)
    ks = iter(jax.random.split(key, 64))
    dp = DEPTH
    bw = LRU_WIDTH // LRU_BLOCKS

    def nrm(shape, scale):
        return jax.random.normal(next(ks), shape, F32) * scale

    def unif(shape, lo, hi):
        return jax.random.uniform(next(ks), shape, F32, lo, hi)

    x_prompt = nrm((BATCH, SEQ, D_MODEL), 1.0)
    x_sample = nrm((DEC_BATCH, DEC_SEQ, D_MODEL), 1.0)
    state_ret = nrm((dp, DEC_BATCH, RET_HEADS, RET_DK, RET_DV), 0.3)
    state_lru = nrm((dp, DEC_BATCH, LRU_WIDTH), 0.5)
    cache_lru_conv = nrm((dp, DEC_BATCH, LRU_CONV - 1, LRU_WIDTH), 1.0)
    state_ssm = nrm((dp, DEC_BATCH, SSD_HEADS, SSD_HEADDIM, SSD_STATE), 0.3)
    cache_ssm_conv = nrm((dp, DEC_BATCH, SSD_CONV - 1, SSD_CONV_DIM), 1.0)
    state_rwkv = nrm((dp, DEC_BATCH, RWKV_HEADS, RWKV_HEADDIM, RWKV_HEADDIM), 0.3)
    cache_rwkv_shift = nrm((dp, DEC_BATCH, 1, COLS_D), 1.0)
    w_in = nrm((dp, D_MODEL, D_IN), D_MODEL ** -0.5)
    w_out = nrm((dp, D_MIX, D_MODEL), D_MIX ** -0.5 * DN_BETA)
    ret_norm_w = 1.0 + nrm((dp, A_V), 0.02)
    lru_conv_w = nrm((dp, LRU_CONV, LRU_WIDTH), LRU_CONV ** -0.5)
    lru_conv_b = nrm((dp, LRU_WIDTH), 0.02)
    lru_wa = nrm((dp, LRU_BLOCKS, bw, bw), bw ** -0.5)
    lru_ba = nrm((dp, LRU_WIDTH), 0.02)
    lru_wx = nrm((dp, LRU_BLOCKS, bw, bw), bw ** -0.5)
    lru_bx = nrm((dp, LRU_WIDTH), 0.02)
    a_base = unif((dp, LRU_WIDTH), 0.9, 0.999) ** (1.0 / LRU_C)
    lru_lambda = jnp.log(a_base) - jnp.log1p(-a_base)
    lru_norm_w = 1.0 + nrm((dp, LRU_WIDTH), 0.02)
    ssd_conv_w = nrm((dp, SSD_CONV, SSD_CONV_DIM), SSD_CONV ** -0.5)
    ssd_conv_b = nrm((dp, SSD_CONV_DIM), 0.02)
    dt0 = jnp.exp(unif((dp, SSD_HEADS), math.log(1e-3), math.log(1e-1)))
    ssd_dt_bias = dt0 + jnp.log(-jnp.expm1(-dt0))
    ssd_A_log = jnp.log(unif((dp, SSD_HEADS), 1.0, 16.0))
    ssd_D = 1.0 + nrm((dp, SSD_HEADS), 0.1)
    ssd_norm_w = 1.0 + nrm((dp, SSD_INNER), 0.02)
    rwkv_mu = unif((dp, COLS_D), 0.2, 0.8)
    ramp = jnp.linspace(0.0, 1.0, RWKV_WIDTH, dtype=F32)
    rwkv_w0 = -5.5 + 5.0 * ramp ** 0.9 + nrm((dp, RWKV_WIDTH), 0.1)
    rwkv_w2 = nrm((dp, RWKV_DECAY_LORA, RWKV_WIDTH), 0.1)
    rwkv_a0 = nrm((dp, RWKV_WIDTH), 0.1)
    rwkv_a2 = nrm((dp, RWKV_A_LORA, RWKV_WIDTH), RWKV_A_LORA ** -0.5)
    rwkv_g2 = nrm((dp, RWKV_GATE_LORA, RWKV_WIDTH), RWKV_GATE_LORA ** -0.5)
    rwkv_k_k = 0.85 + nrm((dp, RWKV_WIDTH), 0.02)
    rwkv_k_a = 1.0 + nrm((dp, RWKV_WIDTH), 0.02)
    rwkv_r_k = nrm((dp, RWKV_HEADS, RWKV_HEADDIM), 0.1)
    rwkv_ln_w = 1.0 + nrm((dp, RWKV_WIDTH), 0.02)
    rwkv_ln_b = nrm((dp, RWKV_WIDTH), 0.02)
    ln1_g = 1.0 + nrm((dp, D_MODEL), 0.02)
    ln1_b = nrm((dp, D_MODEL), 0.02)
    router_w = nrm((dp, D_MODEL, N_EXPERTS), D_MODEL ** -0.5)
    router_bias = nrm((dp, N_EXPERTS), 0.01)
    exp_w_gate = nrm((dp, N_EXPERTS, D_MODEL, D_EXPERT), D_MODEL ** -0.5)
    exp_w_up = nrm((dp, N_EXPERTS, D_MODEL, D_EXPERT), D_MODEL ** -0.5)
    exp_w_down = nrm((dp, N_EXPERTS, D_EXPERT, D_MODEL), D_EXPERT ** -0.5 * DN_BETA)
    sh_w_gate = nrm((dp, D_MODEL, D_SHARED), D_MODEL ** -0.5)
    sh_w_up = nrm((dp, D_MODEL, D_SHARED), D_MODEL ** -0.5)
    sh_w_down = nrm((dp, D_SHARED, D_MODEL), D_SHARED ** -0.5 * DN_BETA)
    ln2_g = 1.0 + nrm((dp, D_MODEL), 0.02)
    ln2_b = nrm((dp, D_MODEL), 0.02)
    return {'x_prompt': x_prompt, 'x_sample': x_sample,
            'state_ret': state_ret, 'state_lru': state_lru, 'cache_lru_conv': cache_lru_conv,
            'state_ssm': state_ssm, 'cache_ssm_conv': cache_ssm_conv,
            'state_rwkv': state_rwkv, 'cache_rwkv_shift': cache_rwkv_shift,
            'w_in': w_in, 'w_out': w_out, 'ret_norm_w': ret_norm_w,
            'lru_conv_w': lru_conv_w, 'lru_conv_b': lru_conv_b, 'lru_wa': lru_wa, 'lru_ba': lru_ba,
            'lru_wx': lru_wx, 'lru_bx': lru_bx, 'lru_lambda': lru_lambda, 'lru_norm_w': lru_norm_w,
            'ssd_conv_w': ssd_conv_w, 'ssd_conv_b': ssd_conv_b, 'ssd_dt_bias': ssd_dt_bias,
            'ssd_A_log': ssd_A_log, 'ssd_D': ssd_D, 'ssd_norm_w': ssd_norm_w,
            'rwkv_mu': rwkv_mu, 'rwkv_w0': rwkv_w0, 'rwkv_w2': rwkv_w2, 'rwkv_a0': rwkv_a0,
            'rwkv_a2': rwkv_a2, 'rwkv_g2': rwkv_g2, 'rwkv_k_k': rwkv_k_k, 'rwkv_k_a': rwkv_k_a,
            'rwkv_r_k': rwkv_r_k, 'rwkv_ln_w': rwkv_ln_w, 'rwkv_ln_b': rwkv_ln_b,
            'ln1_g': ln1_g, 'ln1_b': ln1_b, 'router_w': router_w, 'router_bias': router_bias,
            'exp_w_gate': exp_w_gate, 'exp_w_up': exp_w_up, 'exp_w_down': exp_w_down,
            'sh_w_gate': sh_w_gate, 'sh_w_up': sh_w_up, 'sh_w_down': sh_w_down,
            'ln2_g': ln2_g, 'ln2_b': ln2_b}


def reference(x_prompt, x_sample, state_ret, state_lru, cache_lru_conv, state_ssm, cache_ssm_conv,
              state_rwkv, cache_rwkv_shift, w_in, w_out, ret_norm_w, lru_conv_w, lru_conv_b, lru_wa,
              lru_ba, lru_wx, lru_bx, lru_lambda, lru_norm_w, ssd_conv_w, ssd_conv_b, ssd_dt_bias,
              ssd_A_log, ssd_D, ssd_norm_w, rwkv_mu, rwkv_w0, rwkv_w2, rwkv_a0, rwkv_a2, rwkv_g2,
              rwkv_k_k, rwkv_k_a, rwkv_r_k, rwkv_ln_w, rwkv_ln_b, ln1_g, ln1_b, router_w, router_bias,
              exp_w_gate, exp_w_up, exp_w_down, sh_w_gate, sh_w_up, sh_w_down, ln2_g, ln2_b):
    bp = x_prompt.shape[0]
    dtp = x_prompt.dtype
    pos_p = jnp.arange(x_prompt.shape[1], dtype=jnp.int32)
    pos_s = PAST_LEN + jnp.arange(x_sample.shape[1], dtype=jnp.int32)
    names = ('ret', 'lru_h', 'lru_conv', 'ssd', 'ssd_conv', 'rwkv', 'rwkv_shift')
    out_p = {n: [] for n in names}
    out_s = {n: [] for n in names}
    xp, xs = x_prompt, x_sample
    for l in range(DEPTH):
        p = {'w_in': w_in[l], 'w_out': w_out[l], 'ret_norm_w': ret_norm_w[l],
             'lru_conv_w': lru_conv_w[l], 'lru_conv_b': lru_conv_b[l], 'lru_wa': lru_wa[l], 'lru_ba': lru_ba[l],
             'lru_wx': lru_wx[l], 'lru_bx': lru_bx[l], 'lru_lambda': lru_lambda[l], 'lru_norm_w': lru_norm_w[l],
             'ssd_conv_w': ssd_conv_w[l], 'ssd_conv_b': ssd_conv_b[l], 'ssd_dt_bias': ssd_dt_bias[l],
             'ssd_A_log': ssd_A_log[l], 'ssd_D': ssd_D[l], 'ssd_norm_w': ssd_norm_w[l],
             'rwkv_mu': rwkv_mu[l], 'rwkv_w0': rwkv_w0[l], 'rwkv_w2': rwkv_w2[l], 'rwkv_a0': rwkv_a0[l],
             'rwkv_a2': rwkv_a2[l], 'rwkv_g2': rwkv_g2[l], 'rwkv_k_k': rwkv_k_k[l], 'rwkv_k_a': rwkv_k_a[l],
             'rwkv_r_k': rwkv_r_k[l], 'rwkv_ln_w': rwkv_ln_w[l], 'rwkv_ln_b': rwkv_ln_b[l],
             'ln1_g': ln1_g[l], 'ln1_b': ln1_b[l], 'router_w': router_w[l], 'router_bias': router_bias[l],
             'exp_w_gate': exp_w_gate[l], 'exp_w_up': exp_w_up[l], 'exp_w_down': exp_w_down[l],
             'sh_w_gate': sh_w_gate[l], 'sh_w_up': sh_w_up[l], 'sh_w_down': sh_w_down[l],
             'ln2_g': ln2_g[l], 'ln2_b': ln2_b[l]}
        st_p = {'ret': jnp.zeros((bp,) + state_ret.shape[2:], dtp),
                'lru_h': jnp.zeros((bp,) + state_lru.shape[2:], dtp),
                'lru_conv': jnp.zeros((bp,) + cache_lru_conv.shape[2:], dtp),
                'ssd': jnp.zeros((bp,) + state_ssm.shape[2:], dtp),
                'ssd_conv': jnp.zeros((bp,) + cache_ssm_conv.shape[2:], dtp),
                'rwkv': jnp.zeros((bp,) + state_rwkv.shape[2:], dtp),
                'rwkv_shift': jnp.zeros((bp,) + cache_rwkv_shift.shape[2:], dtp)}
        st_s = {'ret': state_ret[l], 'lru_h': state_lru[l], 'lru_conv': cache_lru_conv[l],
                'ssd': state_ssm[l], 'ssd_conv': cache_ssm_conv[l], 'rwkv': state_rwkv[l],
                'rwkv_shift': cache_rwkv_shift[l]}
        xp, new_p = _layer(xp, pos_p, st_p, p)
        xs, new_s = _layer(xs, pos_s, st_s, p)
        for n in names:
            out_p[n].append(new_p[n].astype(dtp))
            out_s[n].append(new_s[n].astype(x_sample.dtype))
    stk = lambda lst: jnp.stack(lst, axis=0)
    return (xp, xs,
            stk(out_p['ret']), stk(out_s['ret']),
            stk(out_p['lru_h']), stk(out_s['lru_h']),
            stk(out_p['lru_conv']), stk(out_s['lru_conv']),
            stk(out_p['ssd']), stk(out_s['ssd']),
            stk(out_p['ssd_conv']), stk(out_s['ssd_conv']),
            stk(out_p['rwkv']), stk(out_s['rwkv']),
            stk(out_p['rwkv_shift']), stk(out_s['rwkv_shift']))
```

```python
import functools
import math

import numpy as np
import jax
import jax.numpy as jnp
from jax import lax
from jax.experimental import pallas as pl
from jax.experimental.pallas import tpu as pltpu

F32 = jnp.float32
BF16 = jnp.bfloat16
I32 = jnp.int32

D_MODEL = 2048
PAST_LEN = 1024
RET_HEADS, RET_DK, RET_DV = 4, 64, 128
LRU_WIDTH, LRU_BLOCKS, LRU_C = 512, 4, 8.0
SSD_INNER, SSD_HEADDIM, SSD_HEADS, SSD_GROUPS, SSD_STATE = 512, 64, 8, 2, 128
RWKV_WIDTH, RWKV_HEADDIM, RWKV_HEADS = 512, 64, 8
RWKV_LN_EPS = 64e-5
N_EXPERTS, TOP_K, N_EXPERT_GROUPS, TOPK_GROUPS = 64, 8, 8, 4
D_EXPERT = 512
ROUTED_SCALE = 2.5
LN_EPS = 1e-5
RMS_EPS = 1e-6
A_QK = RET_HEADS * RET_DK
A_V = RET_HEADS * RET_DV
COLS_A = 2 * A_QK + 2 * A_V
COLS_B = 2 * LRU_WIDTH
SSD_CONV_DIM = SSD_INNER + 2 * SSD_GROUPS * SSD_STATE
COLS_C = SSD_INNER + SSD_CONV_DIM + SSD_HEADS
COLS_D = 3 * RWKV_WIDTH + 64 + 64 + 128

VMEM_LIMIT_BYTES = 56 * 1024 * 1024
LANES = 128
SUBLANES = 8

TOK_TILE = 512
LN_TILE = 256
MOE_ROWS = 512
COMBINE_TILE = 64
ROUTER_TILE = 256
RET_CHUNK = 256
LRU_CHUNK = 256
SSD_CHUNK = 128
RWKV_CHUNK = 128
RWKV_PRE_TILE = 256
RWKV_UNROLL = 8

HIGHEST = lax.Precision.HIGHEST


def _cp(sem, vmem=VMEM_LIMIT_BYTES):
    return pltpu.CompilerParams(dimension_semantics=sem, vmem_limit_bytes=vmem)


def _sds(shape, dtype):
    return jax.ShapeDtypeStruct(shape, dtype)


def _full(shape):
    nd = len(shape)
    return pl.BlockSpec(shape, lambda *_: (0,) * nd)


def _tile(n, pref, mult=16):
    for t in range(min(pref, n) // mult * mult, 0, -mult):
        if n % t == 0:
            return t
    return n


def _sigmoid(x):
    return jax.nn.sigmoid(x)


def _silu(x):
    return x * jax.nn.sigmoid(x)


def _softplus(x):
    return jnp.maximum(x, 0.0) + jnp.log1p(jnp.exp(-jnp.abs(x)))


class _Dots:
    def __init__(self, precise):
        self.precise = precise

    def _dg(self, a, b, ca, cb):
        if self.precise:
            return lax.dot_general(a, b, (((ca,), (cb,)), ((), ())), preferred_element_type=F32, precision=HIGHEST)
        return lax.dot_general(a.astype(BF16), b.astype(BF16), (((ca,), (cb,)), ((), ())),
                               preferred_element_type=F32)

    def nn(self, a, b):
        return self._dg(a, b, 1, 0)

    def nt(self, a, b):
        return self._dg(a, b, 1, 1)

    def tn(self, a, b):
        return self._dg(a, b, 0, 0)


def _fdot(a, b):
    return jnp.dot(a, b, preferred_element_type=F32, precision=HIGHEST)


def _mm_body(x_ref, w_ref, o_ref):
    o_ref[...] = jnp.dot(x_ref[...], w_ref[...], preferred_element_type=F32)


def _matmul(x, w, rows):
    k = x.shape[1]
    n = w.shape[1]
    tm = _tile(rows, TOK_TILE)
    return pl.pallas_call(
        _mm_body, grid=(rows // tm,),
        in_specs=[pl.BlockSpec((tm, k), lambda i: (i, 0)), _full((k, n))],
        out_specs=pl.BlockSpec((tm, n), lambda i: (i, 0)),
        out_shape=_sds((rows, n), F32), compiler_params=_cp(("parallel",)), name="proj_matmul")(x, w)


def _mm_f32_body(x_ref, w_ref, o_ref):
    o_ref[...] = _fdot(x_ref[...], w_ref[...])


def _matmul_f32(x, w):
    m, k = x.shape
    n = w.shape[1]
    tn = _tile(n, 2 * LANES, LANES)
    return pl.pallas_call(
        _mm_f32_body, grid=(n // tn,),
        in_specs=[_full((m, k)), pl.BlockSpec((k, tn), lambda j: (0, j))],
        out_specs=pl.BlockSpec((m, tn), lambda j: (0, j)),
        out_shape=_sds((m, n), F32), compiler_params=_cp(("parallel",)), name="proj_matmul_f32")(x, w)


def _ret_body(q_ref, k_ref, v_ref, g_ref, cos_ref, sin_ref, dmask_ref, qdec_ref, kdec_ref, cdec_ref,
              nw_ref, s0_ref, o_ref, sout_ref, s_scr, *, precise):
    c = pl.program_id(1)
    dot = _Dots(precise)

    @pl.when(c == 0)
    def _():
        s_scr[...] = s0_ref[0]

    t = q_ref.shape[0]
    lane = lax.broadcasted_iota(I32, (t, LANES), 1)
    first = (lane % RET_DK) < (RET_DK // 2)
    cos = cos_ref[...]
    sin = sin_ref[...]

    def rot(x):
        other = jnp.where(first, pltpu.roll(x, LANES - RET_DK // 2, 1), pltpu.roll(x, RET_DK // 2, 1))
        return x * cos + other * sin

    qs = [rot(q_ref[:, p * LANES:(p + 1) * LANES]) for p in range(2)]
    ks = [rot(k_ref[:, p * LANES:(p + 1) * LANES]) * (RET_DK ** -0.5) for p in range(2)]
    kds = [ks[p] * kdec_ref[p] for p in range(2)]
    for h in range(RET_HEADS):
        lo = (h % 2) * RET_DK
        qh = qs[h // 2][:, lo:lo + RET_DK]
        kh = ks[h // 2][:, lo:lo + RET_DK]
        kdh = kds[h // 2][:, lo:lo + RET_DK]
        vh = v_ref[:, h * RET_DV:(h + 1) * RET_DV]
        s = s_scr[h]
        att = dot.nt(qh, kh) * dmask_ref[h]
        o = dot.nn(att, vh) + dot.nn(qh, s) * qdec_ref[h]
        s_scr[h] = s * cdec_ref[h] + dot.tn(kdh, vh)
        on = o * lax.rsqrt(jnp.mean(o * o, axis=-1, keepdims=True) + RMS_EPS)
        sl = slice(h * RET_DV, (h + 1) * RET_DV)
        o_ref[:, sl] = (on * nw_ref[:, sl] * _silu(g_ref[:, sl])).astype(o_ref.dtype)

    @pl.when(c == pl.num_programs(1) - 1)
    def _():
        sout_ref[0] = s_scr[...]


def _ret_tables(seq, t, pos0):
    half = RET_DK // 2
    freq = 10000.0 ** (-jnp.linspace(0.0, 1.0, half, dtype=F32))
    pos = (pos0 + jnp.arange(seq, dtype=jnp.int32)).astype(F32)
    ang = pos[:, None] * freq
    cos = jnp.cos(ang)
    sin = jnp.sin(ang)
    cos_t = jnp.tile(cos, (1, LANES // half))
    sin_t = jnp.tile(jnp.concatenate([-sin, sin], axis=1), (1, LANES // RET_DK))
    lg = jnp.log1p(-(2.0 ** (-5.0 - jnp.arange(RET_HEADS, dtype=F32))))
    idx = jnp.arange(t, dtype=F32)
    rel = idx[:, None] - idx[None, :]
    causal = rel >= 0
    dmask = jnp.where(causal[None], jnp.exp(jnp.where(causal, rel, 0.0)[None] * lg[:, None, None]), 0.0)
    q_dec = jnp.exp((idx[None, :] + 1.0) * lg[:, None])
    k_dec = jnp.exp((t - 1.0 - idx[None, :]) * lg[:, None])
    c_dec = jnp.exp(t * lg)
    qdec = jnp.broadcast_to(q_dec[:, :, None], (RET_HEADS, t, RET_DV))
    kdec = jnp.repeat(k_dec.T, RET_DK, axis=1).reshape(t, 2, LANES).transpose(1, 0, 2)
    cdec = jnp.broadcast_to(c_dec[:, None, None], (RET_HEADS, 1, RET_DV))
    return cos_t, sin_t, dmask, qdec, kdec, cdec


def _retention(proj_a, s0, norm_w, bsz, seq, pos0, precise):
    t = min(RET_CHUNK, seq)
    nc = seq // t
    cos_t, sin_t, dmask, qdec, kdec, cdec = _ret_tables(seq, t, pos0)

    def rows(w, col):
        return pl.BlockSpec((t, w), lambda b, c: (b * nc + c, col))

    o, s_out = pl.pallas_call(
        functools.partial(_ret_body, precise=precise), grid=(bsz, nc),
        in_specs=[rows(A_QK, 0), rows(A_QK, 1), rows(A_V, 1), rows(A_V, 2),
                  pl.BlockSpec((t, LANES), lambda b, c: (c, 0)),
                  pl.BlockSpec((t, LANES), lambda b, c: (c, 0)),
                  _full((RET_HEADS, t, t)), _full((RET_HEADS, t, RET_DV)), _full((2, t, LANES)),
                  _full((RET_HEADS, 1, RET_DV)), _full((1, A_V)),
                  pl.BlockSpec((1, RET_HEADS, RET_DK, RET_DV), lambda b, c: (b, 0, 0, 0))],
        out_specs=[pl.BlockSpec((t, A_V), lambda b, c: (b * nc + c, 0)),
                   pl.BlockSpec((1, RET_HEADS, RET_DK, RET_DV), lambda b, c: (b, 0, 0, 0))],
        out_shape=[_sds((bsz * seq, A_V), F32 if precise else BF16), _sds((bsz, RET_HEADS, RET_DK, RET_DV), F32)],
        scratch_shapes=[pltpu.VMEM((RET_HEADS, RET_DK, RET_DV), F32)],
        compiler_params=_cp(("parallel", "arbitrary")), name="retention",
    )(proj_a, proj_a, proj_a, proj_a, cos_t, sin_t, dmask, qdec, kdec, cdec, norm_w.reshape(1, A_V), s0)
    return o, s_out


def _conv_chunk(x, xp_scr, cw_ref, cb_ref, t):
    kw = cw_ref.shape[0]
    xp_scr[SUBLANES:SUBLANES + t, :] = x
    y = cb_ref[...] + xp_scr[SUBLANES - kw + 1:SUBLANES - kw + 1 + t, :] * cw_ref[0:1, :]
    for j in range(1, kw):
        off = SUBLANES - kw + 1 + j
        y = y + xp_scr[off:off + t, :] * cw_ref[j:j + 1, :]
    xp_scr[0:SUBLANES, :] = xp_scr[t:t + SUBLANES, :]
    return y


def _pad_conv_state(buf):
    return jnp.pad(buf, ((0, 0), (SUBLANES - buf.shape[1], 0), (0, 0)))


def _lru_body(xb_ref, gb_ref, cbuf_ref, h0_ref, cw_ref, cb_ref, wa_ref, ba_ref, wx_ref, bx_ref, lam_ref,
              nw_ref, o_ref, hout_ref, xp_scr, h_scr, *, first_pos_is_zero, precise):
    c = pl.program_id(1)
    dot = _Dots(precise)
    t = xb_ref.shape[0]

    @pl.when(c == 0)
    def _():
        xp_scr[0:SUBLANES, :] = cbuf_ref[0]
        h_scr[...] = h0_ref[0]

    xc = _conv_chunk(xb_ref[...], xp_scr, cw_ref, cb_ref, t)
    bw = LRU_WIDTH // LRU_BLOCKS
    ra, ri = [], []
    for n in range(LRU_BLOCKS):
        xh = xc[:, n * bw:(n + 1) * bw]
        ra.append(dot.nn(xh, wa_ref[n]))
        ri.append(dot.nn(xh, wx_ref[n]))
    r = _sigmoid(jnp.concatenate(ra, axis=1) + ba_ref[...])
    i = _sigmoid(jnp.concatenate(ri, axis=1) + bx_ref[...])
    log_a = -LRU_C * r * _softplus(-lam_ref[...])
    a = jnp.exp(log_a)
    th = jnp.tanh(log_a)
    mult = jnp.sqrt(-2.0 * th / (1.0 - th))
    row = lax.broadcasted_iota(I32, (t, LRU_WIDTH), 0)
    if first_pos_is_zero:
        mult = jnp.where((row + c * t) == 0, 1.0, mult)
    b = mult * (i * xc)
    s = 1
    while s < t:
        keep = row >= s
        a_sh = jnp.where(keep, pltpu.roll(a, s, 0), 1.0)
        b_sh = jnp.where(keep, pltpu.roll(b, s, 0), 0.0)
        b = a * b_sh + b
        a = a * a_sh
        s *= 2
    h = a * h_scr[...] + b
    h_scr[...] = h[t - 1:t, :]
    gb = gb_ref[...]
    gelu = 0.5 * gb * (1.0 + jnp.tanh(math.sqrt(2.0 / math.pi) * (gb + 0.044715 * (gb * gb * gb))))
    y = h * gelu
    yn = y * lax.rsqrt(jnp.mean(y * y, axis=-1, keepdims=True) + RMS_EPS)
    o_ref[...] = (yn * nw_ref[...]).astype(o_ref.dtype)

    @pl.when(c == pl.num_programs(1) - 1)
    def _():
        hout_ref[0] = h_scr[...]


def _rg_lru(proj_b, cbuf, h0, p, bsz, seq, pos0, precise):
    t = min(LRU_CHUNK, seq)
    nc = seq // t
    w = LRU_WIDTH
    bw = w // LRU_BLOCKS
    row1 = lambda v: v.reshape(1, w)
    o, h_out = pl.pallas_call(
        functools.partial(_lru_body, first_pos_is_zero=(pos0 == 0), precise=precise), grid=(bsz, nc),
        in_specs=[pl.BlockSpec((t, w), lambda b, c: (b * nc + c, 0)),
                  pl.BlockSpec((t, w), lambda b, c: (b * nc + c, 1)),
                  pl.BlockSpec((1, SUBLANES, w), lambda b, c: (b, 0, 0)),
                  pl.BlockSpec((1, 1, w), lambda b, c: (b, 0, 0)),
                  _full((4, w)), _full((1, w)), _full((LRU_BLOCKS, bw, bw)), _full((1, w)),
                  _full((LRU_BLOCKS, bw, bw)), _full((1, w)), _full((1, w)), _full((1, w))],
        out_specs=[pl.BlockSpec((t, w), lambda b, c: (b * nc + c, 0)),
                   pl.BlockSpec((1, 1, w), lambda b, c: (b, 0, 0))],
        out_shape=[_sds((bsz * seq, w), F32 if precise else BF16), _sds((bsz, 1, w), F32)],
        scratch_shapes=[pltpu.VMEM((t + SUBLANES, w), F32), pltpu.VMEM((1, w), F32)],
        compiler_params=_cp(("parallel", "arbitrary")), name="rg_lru",
    )(proj_b, proj_b, _pad_conv_state(cbuf), h0.reshape(bsz, 1, w), p['lru_conv_w'], row1(p['lru_conv_b']),
      p['lru_wa'], row1(p['lru_ba']), p['lru_wx'], row1(p['lru_bx']),
      row1(p['lru_lambda']), row1(p['lru_norm_w']))
    return o, h_out.reshape(bsz, w)


def _ssd_body(z_ref, xs_ref, bc_ref, dt_ref, dtt_ref, cbuf_ref, s0_ref, cw_ref, cb_ref, dtb_ref, dtbc_ref,
              arow_ref, acol_ref, dexp_ref, nw_ref, tril_ref, triu_ref, o_ref, sout_ref, xp_scr, st_scr, *, precise):
    c = pl.program_id(1)
    dot = _Dots(precise)
    t = z_ref.shape[0]
    npair = SSD_HEADS // 2

    @pl.when(c == 0)
    def _():
        xp_scr[0:SUBLANES, :] = cbuf_ref[0]
        for p in range(npair):
            st_scr[p] = s0_ref[0, 2 * p:2 * p + 2].reshape(LANES, SSD_STATE).T

    xin = jnp.concatenate([xs_ref[...], bc_ref[...]], axis=1)
    xbc = _silu(_conv_chunk(xin, xp_scr, cw_ref, cb_ref, t))
    xs = xbc[:, :SSD_INNER]
    gs = SSD_GROUPS * SSD_STATE
    bm = [xbc[:, SSD_INNER + g * SSD_STATE:SSD_INNER + (g + 1) * SSD_STATE] for g in range(SSD_GROUPS)]
    cm = [xbc[:, SSD_INNER + gs + g * SSD_STATE:SSD_INNER + gs + (g + 1) * SSD_STATE] for g in range(SSD_GROUPS)]
    dt_c = _softplus(dt_ref[...] + dtb_ref[...])
    dt_r = _softplus(dtt_ref[0] + dtbc_ref[:, :t])
    cum_c = _fdot(tril_ref[...], dt_c * arow_ref[...])
    cum_r = _fdot(dt_r * acol_ref[:, :t], triu_ref[...])
    cb = [dot.nt(cm[g], bm[g]) for g in range(SSD_GROUPS)]
    rowi = lax.broadcasted_iota(I32, (t, t), 0)
    coli = lax.broadcasted_iota(I32, (t, t), 1)
    tri = rowi >= coli
    lane = lax.broadcasted_iota(I32, (t, LANES), 1)
    lo = lane < SSD_HEADDIM
    ys = []
    for p in range(npair):
        g = (2 * p) // (SSD_HEADS // SSD_GROUPS)
        x_pair = xs[:, p * LANES:(p + 1) * LANES]
        y_pair = jnp.zeros((t, LANES), F32)
        cc, cend, te = [], [], []
        for hh in range(2):
            h = 2 * p + hh
            cch = cum_c[:, h:h + 1]
            seg = cch - cum_r[h:h + 1, :]
            lmat = jnp.where(tri, jnp.exp(jnp.where(tri, seg, 0.0)), 0.0)
            m = cb[g] * lmat * dt_r[h:h + 1, :]
            xm = jnp.where(lo if hh == 0 else jnp.logical_not(lo), x_pair, 0.0)
            y_pair = y_pair + dot.nn(m, xm)
            ce = cum_c[t - 1:t, h:h + 1]
            cc.append(cch)
            cend.append(ce)
            te.append(jnp.exp(ce - cch) * dt_c[:, h:h + 1])
        st = st_scr[p]
        y_pair = y_pair + dot.nn(cm[g], st) * jnp.where(lo, jnp.exp(cc[0]), jnp.exp(cc[1]))
        xt = x_pair * jnp.where(lo, te[0], te[1])
        st_scr[p] = st * jnp.where(lo[0:1, :], jnp.exp(cend[0]), jnp.exp(cend[1])) + dot.tn(bm[g], xt)
        ys.append(y_pair)
    y = jnp.concatenate(ys, axis=1)
    z = z_ref[...]
    y = (y + xs * dexp_ref[...]) * _silu(z)
    gw = SSD_INNER // SSD_GROUPS
    outs = []
    for g in range(SSD_GROUPS):
        yg = y[:, g * gw:(g + 1) * gw]
        outs.append(yg * lax.rsqrt(jnp.mean(yg * yg, axis=-1, keepdims=True) + RMS_EPS))
    o_ref[...] = (jnp.concatenate(outs, axis=1) * nw_ref[...]).astype(o_ref.dtype)

    @pl.when(c == pl.num_programs(1) - 1)
    def _():
        for p in range(npair):
            sout_ref[0, 2 * p:2 * p + 2] = st_scr[p].T.reshape(2, SSD_HEADDIM, SSD_STATE)


def _ssd(proj_c, dt_raw, cbuf, s0, p, bsz, seq, precise):
    t = min(SSD_CHUNK, seq)
    nc = seq // t
    w = SSD_INNER
    cw = SSD_CONV_DIM
    dtt = jnp.swapaxes(dt_raw[:, :SSD_HEADS].reshape(bsz, seq, SSD_HEADS), 1, 2)
    dtb = p['ssd_dt_bias'].astype(F32)
    a = -jnp.exp(p['ssd_A_log'].astype(F32))
    pad8 = lambda v: jnp.pad(v, (0, LANES - SSD_HEADS)).reshape(1, LANES)
    col8 = lambda v: jnp.broadcast_to(v[:, None], (SSD_HEADS, LANES))
    dexp = jnp.repeat(p['ssd_D'].astype(F32), SSD_HEADDIM).reshape(1, w)
    tril = jnp.tril(jnp.ones((t, t), F32))
    o, s_out = pl.pallas_call(
        functools.partial(_ssd_body, precise=precise), grid=(bsz, nc),
        in_specs=[pl.BlockSpec((t, w), lambda b, c: (b * nc + c, 0)),
                  pl.BlockSpec((t, w), lambda b, c: (b * nc + c, 1)),
                  pl.BlockSpec((t, w), lambda b, c: (b * nc + c, 2)),
                  pl.BlockSpec((t, LANES), lambda b, c: (b * nc + c, 0)),
                  pl.BlockSpec((1, SSD_HEADS, t), lambda b, c: (b, 0, c)),
                  pl.BlockSpec((1, SUBLANES, cw), lambda b, c: (b, 0, 0)),
                  pl.BlockSpec((1, SSD_HEADS, SSD_HEADDIM, SSD_STATE), lambda b, c: (b, 0, 0, 0)),
                  _full((4, cw)), _full((1, cw)), _full((1, LANES)), _full((SSD_HEADS, LANES)),
                  _full((1, LANES)), _full((SSD_HEADS, LANES)), _full((1, w)), _full((1, w)),
                  _full((t, t)), _full((t, t))],
        out_specs=[pl.BlockSpec((t, w), lambda b, c: (b * nc + c, 0)),
                   pl.BlockSpec((1, SSD_HEADS, SSD_HEADDIM, SSD_STATE), lambda b, c: (b, 0, 0, 0))],
        out_shape=[_sds((bsz * seq, w), F32 if precise else BF16), _sds((bsz, SSD_HEADS, SSD_HEADDIM, SSD_STATE), F32)],
        scratch_shapes=[pltpu.VMEM((t + SUBLANES, cw), F32), pltpu.VMEM((SSD_HEADS // 2, SSD_STATE, LANES), F32)],
        compiler_params=_cp(("parallel", "arbitrary")), name="ssd",
    )(proj_c, proj_c, proj_c, dt_raw, dtt, _pad_conv_state(cbuf), s0, p['ssd_conv_w'],
      p['ssd_conv_b'].reshape(1, cw), pad8(dtb), col8(dtb), pad8(a), col8(a), dexp,
      p['ssd_norm_w'].reshape(1, w), tril, tril.T)
    return o, s_out


def _rwkv_pre_body(pd_ref, prev_ref, shift_ref, mu_ref, wlora_ref, g2_ref, w0_ref, a0_ref, kk_ref, ka_ref,
                   rk_ref, ones_ref, r_ref, w_ref, k_ref, v_ref, nkk_ref, bb_ref, bonus_ref, og_ref, *, precise):
    c = pl.program_id(1)
    dot = _Dots(precise)
    t = pd_ref.shape[0]
    w = RWKV_WIDTH
    pd = pd_ref[...]
    prev_row = jnp.where(c == 0, shift_ref[0], prev_ref[SUBLANES - 1:SUBLANES, :])
    row = lax.broadcasted_iota(I32, pd.shape, 0)
    prev = jnp.where(row == 0, prev_row, pltpu.roll(pd, 1, 0))
    mixed = pd + (prev - pd) * mu_ref[...]
    r = mixed[:, 0:w]
    kd = mixed[:, w:2 * w]
    vd = mixed[:, 2 * w:3 * w]
    lora_in = mixed[:, 3 * w:3 * w + LANES]
    lane = lax.broadcasted_iota(I32, lora_in.shape, 1)
    lora_in = jnp.where(lane < 64, jnp.tanh(lora_in), lora_in)
    lora = dot.nn(lora_in, wlora_ref[...])
    gd = mixed[:, 3 * w + LANES:3 * w + 2 * LANES]
    og_ref[...] = dot.nn(_sigmoid(gd), g2_ref[...])
    w_log = -_softplus(-(w0_ref[...] + lora[:, :w])) - 0.5
    w_ref[...] = jnp.exp(-jnp.exp(w_log))
    iclr = _sigmoid(a0_ref[...] + lora[:, w:])
    kk = kd * kk_ref[...]
    ss = _fdot(kk * kk, ones_ref[...])
    kk = kk / jnp.maximum(jnp.sqrt(ss), 1e-12)
    kr = kd * (1.0 + (iclr - 1.0) * ka_ref[...])
    r_ref[...] = r
    k_ref[...] = kr
    v_ref[...] = vd
    nkk_ref[...] = -kk
    bb_ref[...] = kk * iclr
    bonus_ref[...] = _fdot(r * kr * rk_ref[...], ones_ref[...]) * vd


def _head_ones():
    hid = jnp.arange(RWKV_WIDTH) // RWKV_HEADDIM
    return (hid[:, None] == hid[None, :]).astype(F32)


def _rwkv_pre(proj_d, shift, p, bsz, seq, precise):
    t = min(RWKV_PRE_TILE, seq)
    nc = seq // t
    w = RWKV_WIDTH
    tb = t // SUBLANES
    wlora = jnp.zeros((LANES, 2 * w), F32).at[:64, :w].set(p['rwkv_w2']).at[64:, w:].set(p['rwkv_a2'])
    row1 = lambda v: v.reshape(1, -1)
    outs = pl.pallas_call(
        functools.partial(_rwkv_pre_body, precise=precise), grid=(bsz, nc),
        in_specs=[pl.BlockSpec((t, COLS_D), lambda b, c: (b * nc + c, 0)),
                  pl.BlockSpec((SUBLANES, COLS_D), lambda b, c: (jnp.maximum((b * nc + c) * tb - 1, 0), 0)),
                  pl.BlockSpec((1, 1, COLS_D), lambda b, c: (b, 0, 0)),
                  _full((1, COLS_D)), _full((LANES, 2 * w)), _full((LANES, w)), _full((1, w)), _full((1, w)),
                  _full((1, w)), _full((1, w)), _full((1, w)), _full((w, w))],
        out_specs=[pl.BlockSpec((t, w), lambda b, c: (b * nc + c, 0))] * 8,
        out_shape=[_sds((bsz * seq, w), F32)] * 8,
        compiler_params=_cp(("parallel", "parallel")), name="rwkv_pre",
    )(proj_d, proj_d, shift, row1(p['rwkv_mu']), wlora, p['rwkv_g2'], row1(p['rwkv_w0']),
      row1(p['rwkv_a0']), row1(p['rwkv_k_k']), row1(p['rwkv_k_a']), row1(p['rwkv_r_k']), _head_ones())
    return outs


def _rwkv_scan_body(r_ref, w_ref, k_ref, v_ref, a_ref, b_ref, s0_ref, o_ref, sout_ref, s_scr, vt_scr, ot_scr,
                    o8_scr, *, t):
    c = pl.program_id(1)
    npair = RWKV_HEADS // 2
    nch = 2 * RWKV_HEADS

    @pl.when(c == 0)
    def _():
        s_scr[...] = s0_ref[...].reshape(nch, RWKV_HEADDIM, LANES)

    ot_scr[...] = jnp.zeros(ot_scr.shape, F32)
    o8_scr[...] = jnp.zeros(o8_scr.shape, F32)
    for bb in range(2):
        for p in range(npair):
            if t < LANES:
                ot_scr[bb * npair + p, 0:t, :] = v_ref[bb, :, p * LANES:(p + 1) * LANES]
                vt_scr[bb * npair + p] = ot_scr[bb * npair + p].T
            else:
                vt_scr[bb * npair + p] = v_ref[bb, :, p * LANES:(p + 1) * LANES].T
    ot_scr[...] = jnp.zeros(ot_scr.shape, F32)

    lane_row = lax.broadcasted_iota(I32, (1, LANES), 1)
    half = [lane_row < RWKV_HEADDIM, lane_row >= RWKV_HEADDIM]
    lane_full = lax.broadcasted_iota(I32, (RWKV_HEADDIM, LANES), 1)

    def block(i, carry):
        t0 = pl.multiple_of(i * RWKV_UNROLL, RWKV_UNROLL)
        shift = (LANES - t0) % LANES
        vts = [pltpu.roll(vt_scr[q], shift, 1) for q in range(2 * npair)]
        for j in range(RWKV_UNROLL):
            for bb in range(2):
                for p in range(npair):
                    q = bb * npair + p
                    cols = slice(p * LANES, (p + 1) * LANES)
                    r_row = r_ref[bb, pl.ds(t0, RWKV_UNROLL), cols][j:j + 1, :]
                    w_row = w_ref[bb, pl.ds(t0, RWKV_UNROLL), cols][j:j + 1, :]
                    k_row = k_ref[bb, pl.ds(t0, RWKV_UNROLL), cols][j:j + 1, :]
                    a_row = a_ref[bb, pl.ds(t0, RWKV_UNROLL), cols][j:j + 1, :]
                    b_row = b_ref[bb, pl.ds(t0, RWKV_UNROLL), cols][j:j + 1, :]
                    for hh in range(2):
                        g = 2 * q + hh
                        s = s_scr[g]
                        vcol = vts[q][hh * RWKV_HEADDIM:(hh + 1) * RWKV_HEADDIM, j:j + 1]
                        sa = jnp.sum(s * a_row, axis=1, keepdims=True)
                        s = s * w_row + sa * jnp.where(half[hh], b_row, 0.0) + vcol * jnp.where(half[hh], k_row, 0.0)
                        s_scr[g] = s
                        o8_scr[g, :, j:j + 1] = jnp.sum(s * r_row, axis=1, keepdims=True)
        sel = (lane_full >= t0) & (lane_full < t0 + RWKV_UNROLL)
        for g in range(nch):
            ot_scr_view = ot_scr[g // 2, (g % 2) * RWKV_HEADDIM:(g % 2 + 1) * RWKV_HEADDIM, :]
            ot_scr[g // 2, (g % 2) * RWKV_HEADDIM:(g % 2 + 1) * RWKV_HEADDIM, :] = jnp.where(
                sel, pltpu.roll(o8_scr[g], t0, 1), ot_scr_view)
        return carry

    lax.fori_loop(0, t // RWKV_UNROLL, block, 0)

    for bb in range(2):
        for p in range(npair):
            o_ref[bb, :, p * LANES:(p + 1) * LANES] = ot_scr[bb * npair + p].T[0:t, :]

    @pl.when(c == pl.num_programs(1) - 1)
    def _():
        sout_ref[...] = s_scr[...].reshape(2, RWKV_HEADS, RWKV_HEADDIM, LANES)


def _rwkv_scan(r, w, k, v, nkk, bb, s0, bsz, seq):
    t = min(RWKV_CHUNK, seq)
    nc = seq // t
    wd = RWKV_WIDTH
    hd = RWKV_HEADDIM
    odd = (jnp.arange(RWKV_HEADS) % 2 == 1)[None, :, None, None]
    zeros = jnp.zeros_like(s0)
    s0p = jnp.where(odd, jnp.concatenate([zeros, s0], axis=-1), jnp.concatenate([s0, zeros], axis=-1))
    r3 = lambda x: x.reshape(bsz, seq, wd)
    spec = pl.BlockSpec((2, t, wd), lambda b, c: (b, c, 0))
    sspec = pl.BlockSpec((2, RWKV_HEADS, hd, LANES), lambda b, c: (b, 0, 0, 0))
    o, s_out = pl.pallas_call(
        functools.partial(_rwkv_scan_body, t=t), grid=(bsz // 2, nc),
        in_specs=[spec] * 6 + [sspec], out_specs=[spec, sspec],
        out_shape=[_sds((bsz, seq, wd), F32), _sds((bsz, RWKV_HEADS, hd, LANES), F32)],
        scratch_shapes=[pltpu.VMEM((2 * RWKV_HEADS, hd, LANES), F32), pltpu.VMEM((RWKV_HEADS, LANES, LANES), F32),
                        pltpu.VMEM((RWKV_HEADS, LANES, LANES), F32), pltpu.VMEM((2 * RWKV_HEADS, hd, LANES), F32)],
        compiler_params=_cp(("parallel", "arbitrary")), name="rwkv_scan",
    )(r3(r), r3(w), r3(k), r3(v), r3(nkk), r3(bb), s0p)
    s_fin = jnp.where(odd, s_out[..., hd:], s_out[..., :hd])
    return o.reshape(bsz * seq, wd), s_fin


def _rwkv_post_body(o_ref, bonus_ref, og_ref, lw_ref, lb_ref, ones_ref, out_ref):
    o = o_ref[...]
    inv = 1.0 / RWKV_HEADDIM
    mean = _fdot(o, ones_ref[...]) * inv
    xc = o - mean
    var = _fdot(xc * xc, ones_ref[...]) * inv
    y = xc * lax.rsqrt(var + RWKV_LN_EPS) * lw_ref[...] + lb_ref[...]
    out_ref[...] = ((y + bonus_ref[...]) * og_ref[...]).astype(out_ref.dtype)


def _rwkv_post(o, bonus, og, p, precise):
    n = o.shape[0]
    t = _tile(n, RWKV_PRE_TILE)
    w = RWKV_WIDTH
    spec = pl.BlockSpec((t, w), lambda i: (i, 0))
    return pl.pallas_call(
        _rwkv_post_body, grid=(n // t,),
        in_specs=[spec, spec, spec, _full((1, w)), _full((1, w)), _full((w, w))],
        out_specs=spec, out_shape=_sds((n, w), F32 if precise else BF16),
        compiler_params=_cp(("parallel",)), name="rwkv_post",
    )(o, bonus, og, p['rwkv_ln_w'].reshape(1, w), p['rwkv_ln_b'].reshape(1, w), _head_ones())


def _layer_norm_rows(z, g, b):
    zc = z - jnp.mean(z, axis=-1, keepdims=True)
    return zc * lax.rsqrt(jnp.mean(zc * zc, axis=-1, keepdims=True) + LN_EPS) * g + b


def _out_ln_body(mixp_ref, mixs_ref, w_hbm, x_ref, g_ref, b_ref, y_ref, yb_ref, wf_scr, wb_scr, sem, *, alpha,
                 prompt_tiles):
    i = pl.program_id(0)

    @pl.when(i == 0)
    def _():
        copy = pltpu.make_async_copy(w_hbm, wf_scr, sem.at[0])
        copy.start()
        copy.wait()
        wb_scr[...] = wf_scr[...].astype(BF16)

    def finish(mix):
        y = _layer_norm_rows(alpha * x_ref[...] + mix, g_ref[...], b_ref[...])
        y_ref[...] = y
        yb_ref[...] = y.astype(BF16)

    @pl.when(i < prompt_tiles)
    def _():
        finish(jnp.dot(mixp_ref[...], wb_scr[...], preferred_element_type=F32))

    @pl.when(i >= prompt_tiles)
    def _():
        finish(_fdot(mixs_ref[...], wf_scr[...]))


def _out_ln(mix_p, mix_s, w_out, x, g, b, alpha):
    n, d = x.shape
    n_p, n_s = mix_p.shape[0], mix_s.shape[0]
    tm = _tile(math.gcd(n_p, n_s), LN_TILE)
    pt = n_p // tm
    rows = lambda: pl.BlockSpec((tm, d), lambda i: (i, 0))
    return pl.pallas_call(
        functools.partial(_out_ln_body, alpha=alpha, prompt_tiles=pt), grid=(n // tm,),
        in_specs=[pl.BlockSpec((tm, d), lambda i: (jnp.minimum(i, pt - 1), 0)),
                  pl.BlockSpec((tm, d), lambda i: (jnp.maximum(i - pt, 0), 0)),
                  pl.BlockSpec(memory_space=pl.ANY), rows(), _full((1, d)), _full((1, d))],
        out_specs=[rows(), rows()], out_shape=[_sds((n, d), F32), _sds((n, d), BF16)],
        scratch_shapes=[pltpu.VMEM((d, d), F32), pltpu.VMEM((d, d), BF16), pltpu.SemaphoreType.DMA((1,))],
        compiler_params=_cp(("arbitrary",)), name="out_proj_ln",
    )(mix_p, mix_s, w_out, x, g.reshape(1, d), b.reshape(1, d))


def _router_body(x_ref, wt_ref, bias_ref, tri_ref, idx_ref, gate_ref, rank_ref, cnt_ref, cnt_scr):
    i = pl.program_id(0)
    tm = x_ref.shape[0]
    ne, ng = N_EXPERTS, N_EXPERT_GROUPS
    per = ne // ng

    @pl.when(i == 0)
    def _():
        cnt_scr[...] = jnp.zeros(cnt_scr.shape, F32)

    logits = lax.dot_general(wt_ref[...], x_ref[...], (((1,), (1,)), ((), ())), preferred_element_type=F32,
                             precision=HIGHEST)
    scores = _sigmoid(logits)
    biased = scores + bias_ref[...]
    neg = -jnp.inf
    b3 = biased.reshape(ng, per, tm)
    e3 = lax.broadcasted_iota(I32, (ng, per, tm), 1)
    m1 = jnp.max(b3, axis=1, keepdims=True)
    i1 = jnp.min(jnp.where(b3 == m1, e3, per), axis=1, keepdims=True)
    m2 = jnp.max(jnp.where(e3 == i1, neg, b3), axis=1, keepdims=True)
    grp = (m1 + m2).reshape(ng, tm)
    gi = lax.broadcasted_iota(I32, (ng, tm), 0)
    keep = jnp.zeros((ng, tm), jnp.bool_)
    for _ in range(TOPK_GROUPS):
        m = jnp.max(grp, axis=0, keepdims=True)
        first = jnp.min(jnp.where(grp == m, gi, ng), axis=0, keepdims=True)
        sel = gi == first
        keep = jnp.logical_or(keep, sel)
        grp = jnp.where(sel, neg, grp)
    keep_e = jnp.broadcast_to(keep.astype(F32).reshape(ng, 1, tm), (ng, per, tm)).reshape(ne, tm) > 0.5
    masked = jnp.where(keep_e, biased, neg)
    ei = lax.broadcasted_iota(I32, (ne, tm), 0)
    sels, idxs, gates = [], [], []
    chosen = jnp.zeros((ne, tm), F32)
    for _ in range(TOP_K):
        m = jnp.max(masked, axis=0, keepdims=True)
        first = jnp.min(jnp.where(masked == m, ei, ne), axis=0, keepdims=True)
        sel = ei == first
        idxs.append(first)
        gates.append(jnp.sum(jnp.where(sel, scores, 0.0), axis=0, keepdims=True))
        sels.append(sel)
        chosen = chosen + sel.astype(F32)
        masked = jnp.where(sel, neg, masked)
    gsum = gates[0]
    for k in range(1, TOP_K):
        gsum = gsum + gates[k]
    before = jnp.dot(chosen.astype(BF16), tri_ref[...], preferred_element_type=F32) + cnt_scr[:, 0:tm]
    for k in range(TOP_K):
        idx_ref[k:k + 1, :] = idxs[k]
        gate_ref[k:k + 1, :] = gates[k] / gsum * ROUTED_SCALE
        rank_ref[k:k + 1, :] = jnp.sum(jnp.where(sels[k], before, 0.0), axis=0, keepdims=True).astype(I32)
    cnt_scr[...] = cnt_scr[...] + jnp.sum(chosen, axis=1, keepdims=True)
    cnt_ref[...] = cnt_scr[...]


def _router(x, router_w, router_bias):
    n, d = x.shape
    tm = _tile(n, ROUTER_TILE, LANES)
    tri = (jnp.arange(tm)[:, None] < jnp.arange(tm)[None, :]).astype(BF16)
    bias = jnp.broadcast_to(router_bias.astype(F32)[:, None], (N_EXPERTS, tm))
    tok = lambda: pl.BlockSpec((TOP_K, tm), lambda i: (0, i))
    return pl.pallas_call(
        _router_body, grid=(n // tm,),
        in_specs=[pl.BlockSpec((tm, d), lambda i: (i, 0)), _full((N_EXPERTS, d)), _full((N_EXPERTS, tm)),
                  _full((tm, tm))],
        out_specs=[tok(), tok(), tok(), _full((N_EXPERTS, tm))],
        out_shape=[_sds((TOP_K, n), I32), _sds((TOP_K, n), F32), _sds((TOP_K, n), I32), _sds((N_EXPERTS, tm), F32)],
        scratch_shapes=[pltpu.VMEM((N_EXPERTS, tm), F32)],
        compiler_params=_cp(("arbitrary",)), name="router",
    )(x, router_w.T, bias, tri)


def _ffn(xb, wg, wu, wd):
    h = _silu(jnp.dot(xb, wg, preferred_element_type=F32)) * jnp.dot(xb, wu, preferred_element_type=F32)
    return jnp.dot(h.astype(BF16), wd, preferred_element_type=F32)


def _shared_body(x_ref, wg_ref, wu_ref, wd_ref, o_ref):
    o_ref[...] = _ffn(x_ref[...], wg_ref[...], wu_ref[...], wd_ref[...])


def _shared_ffn(xb, wg, wu, wd):
    n, d = xb.shape
    tm = _tile(n, TOK_TILE)
    f = wg.shape[1]
    return pl.pallas_call(
        _shared_body, grid=(n // tm,),
        in_specs=[pl.BlockSpec((tm, d), lambda i: (i, 0)), _full((d, f)), _full((d, f)), _full((f, d))],
        out_specs=pl.BlockSpec((tm, d), lambda i: (i, 0)), out_shape=_sds((n, d), F32),
        compiler_params=_cp(("parallel",)), name="shared_ffn")(xb, wg, wu, wd)


def _gather_rows_copy(x_hbm, buf, sem, slot, r, tok):
    return pltpu.make_async_copy(x_hbm.at[pl.ds(tok, 1)], buf.at[slot, pl.ds(r, 1)], sem.at[slot])


def _expert_body(be_ref, nu_ref, tok0_ref, tok1_ref, x_hbm, wg_ref, wu_ref, wd_ref, y_ref, xbuf, sem):
    i = pl.program_id(0)
    n_used = nu_ref[0]
    slot = i % 2
    rows = xbuf.shape[1]

    def issue(tok_ref, dst_slot):
        def one(r, carry):
            _gather_rows_copy(x_hbm, xbuf, sem, dst_slot, r, tok_ref[0, 0, r]).start()
            return carry
        lax.fori_loop(0, rows, one, 0, unroll=8)

    @pl.when(jnp.logical_and(i == 0, n_used > 0))
    def _():
        issue(tok0_ref, 0)

    @pl.when(i + 1 < n_used)
    def _():
        issue(tok1_ref, 1 - slot)

    @pl.when(i < n_used)
    def _():
        pltpu.make_async_copy(x_hbm.at[pl.ds(0, rows)], xbuf.at[slot], sem.at[slot]).wait()
        y_ref[...] = _ffn(xbuf[slot].astype(BF16), wg_ref[...], wu_ref[...], wd_ref[...])

    @pl.when(i >= n_used)
    def _():
        y_ref[...] = jnp.zeros(y_ref.shape, F32)


def _expert_ffn(x, row_tok, block_e, n_used, wg, wu, wd):
    n, d = x.shape
    nb = block_e.shape[0]
    rows = MOE_ROWS
    f = wg.shape[2]
    tok3 = row_tok.reshape(nb, 1, rows)
    smem = pltpu.SMEM
    grid_spec = pltpu.PrefetchScalarGridSpec(
        num_scalar_prefetch=2, grid=(nb,),
        in_specs=[pl.BlockSpec((1, 1, rows), lambda i, be, nu: (i, 0, 0), memory_space=smem),
                  pl.BlockSpec((1, 1, rows), lambda i, be, nu: (jnp.minimum(i + 1, nb - 1), 0, 0), memory_space=smem),
                  pl.BlockSpec(memory_space=pl.ANY),
                  pl.BlockSpec((None, d, f), lambda i, be, nu: (be[i], 0, 0)),
                  pl.BlockSpec((None, d, f), lambda i, be, nu: (be[i], 0, 0)),
                  pl.BlockSpec((None, f, d), lambda i, be, nu: (be[i], 0, 0))],
        out_specs=pl.BlockSpec((rows, d), lambda i, be, nu: (i, 0)),
        scratch_shapes=[pltpu.VMEM((2, rows, d), F32), pltpu.SemaphoreType.DMA((2,))])
    return pl.pallas_call(
        _expert_body, grid_spec=grid_spec, out_shape=_sds((nb * rows, d), F32),
        compiler_params=_cp(("arbitrary",)), name="expert_ffn",
    )(block_e, n_used, tok3, tok3, x, wg, wu, wd)


def _combine_copy(y_hbm, buf, sem, slot, j, row):
    return pltpu.make_async_copy(y_hbm.at[pl.ds(row, 1)], buf.at[slot, pl.ds(j, 1)], sem.at[slot])


def _combine_body(d0_ref, d1_ref, y_hbm, gate_ref, sh_ref, x_ref, g_ref, b_ref, o_ref, ob_ref, ybuf, sem, *, alpha):
    i = pl.program_id(0)
    nt = pl.num_programs(0)
    slot = i % 2
    nrow = ybuf.shape[1]
    tc = x_ref.shape[0]

    def issue(d_ref, dst_slot):
        def one(j, carry):
            _combine_copy(y_hbm, ybuf, sem, dst_slot, j, d_ref[0, 0, j]).start()
            return carry
        lax.fori_loop(0, nrow, one, 0, unroll=8)

    @pl.when(i == 0)
    def _():
        issue(d0_ref, 0)

    @pl.when(i + 1 < nt)
    def _():
        issue(d1_ref, 1 - slot)

    pltpu.make_async_copy(y_hbm.at[pl.ds(0, nrow)], ybuf.at[slot], sem.at[slot]).wait()
    gate = gate_ref[...]
    routed = ybuf[slot, 0:tc, :] * gate[:, 0:1]
    for k in range(1, TOP_K):
        routed = routed + ybuf[slot, k * tc:(k + 1) * tc, :] * gate[:, k:k + 1]
    y = _layer_norm_rows(alpha * x_ref[...] + (routed + sh_ref[...]), g_ref[...], b_ref[...])
    o_ref[...] = y
    ob_ref[...] = y.astype(BF16)


def _combine(y_sorted, dest, gate_t, shared, x, g, b, alpha):
    n, d = x.shape
    tc = _tile(n, COMBINE_TILE, SUBLANES)
    nt = n // tc
    dtile = dest.reshape(TOP_K, nt, tc).transpose(1, 0, 2).reshape(nt, 1, TOP_K * tc)
    rows = lambda: pl.BlockSpec((tc, d), lambda i: (i, 0))
    smem = pltpu.SMEM
    return pl.pallas_call(
        functools.partial(_combine_body, alpha=alpha), grid=(nt,),
        in_specs=[pl.BlockSpec((1, 1, TOP_K * tc), lambda i: (i, 0, 0), memory_space=smem),
                  pl.BlockSpec((1, 1, TOP_K * tc), lambda i: (jnp.minimum(i + 1, nt - 1), 0, 0), memory_space=smem),
                  pl.BlockSpec(memory_space=pl.ANY),
                  pl.BlockSpec((tc, TOP_K), lambda i: (i, 0)), rows(), rows(), _full((1, d)), _full((1, d))],
        out_specs=[rows(), rows()], out_shape=[_sds((n, d), F32), _sds((n, d), BF16)],
        scratch_shapes=[pltpu.VMEM((2, TOP_K * tc, d), F32), pltpu.SemaphoreType.DMA((2,))],
        compiler_params=_cp(("arbitrary",)), name="moe_combine",
    )(dtile, dtile, y_sorted, gate_t, shared, x, g.reshape(1, d), b.reshape(1, d))


def _moe_ln(x, xb, p, alpha):
    n, d = x.shape
    idx, gate, rank, cnt = _router(x, p['router_w'], p['router_bias'])
    counts = cnt[:, 0].astype(I32)
    rows = MOE_ROWS
    padded = (counts + rows - 1) // rows * rows
    pad_end = jnp.cumsum(padded)
    pad_start = pad_end - padded
    dest = pad_start[idx] + rank
    nb = (n * TOP_K) // rows + N_EXPERTS
    n_used = (pad_end[-1] // rows).astype(I32).reshape(1)
    block_e = jnp.minimum(jnp.searchsorted(pad_end, jnp.arange(nb, dtype=I32) * rows, side='right'),
                          N_EXPERTS - 1).astype(I32)
    tok = jnp.broadcast_to(jnp.arange(n, dtype=I32)[None, :], (TOP_K, n))
    row_tok = jnp.zeros((nb * rows,), I32).at[dest.reshape(-1)].set(tok.reshape(-1), unique_indices=True)
    y_sorted = _expert_ffn(x, row_tok, block_e, n_used, p['exp_w_gate'], p['exp_w_up'], p['exp_w_down'])
    shared = _shared_ffn(xb, p['sh_w_gate'], p['sh_w_up'], p['sh_w_down'])
    return _combine(y_sorted, dest, gate.T, shared, x, p['ln2_g'], p['ln2_b'], alpha)


def _split_w_in(w_in):
    a0, b0, c0, d0 = 0, COLS_A, COLS_A + COLS_B, COLS_A + COLS_B + COLS_C
    w_dt = jnp.pad(w_in[:, c0 + SSD_INNER + SSD_CONV_DIM:d0], ((0, 0), (0, LANES - SSD_HEADS)))
    return w_in[:, a0:b0], w_in[:, b0:c0], w_in[:, c0:c0 + SSD_INNER + SSD_CONV_DIM], w_dt, w_in[:, d0:]


def _mixers(projs, st, p, bsz, seq, pos0, precise):
    proj_a, proj_b, proj_c, dt_raw, proj_d = projs
    o_a, ret_s = _retention(proj_a, st['ret'], p['ret_norm_w'], bsz, seq, pos0, precise)
    o_b, lru_h = _rg_lru(proj_b, st['lru_conv'], st['lru_h'], p, bsz, seq, pos0, precise)
    o_c, ssd_s = _ssd(proj_c, dt_raw, st['ssd_conv'], st['ssd'], p, bsz, seq, precise)
    r, w, k, v, nkk, bb, bonus, og = _rwkv_pre(proj_d, st['rwkv_shift'], p, bsz, seq, precise)
    o, rwkv_s = _rwkv_scan(r, w, k, v, nkk, bb, st['rwkv'], bsz, seq)
    o_d = _rwkv_post(o, bonus, og, p, precise)
    last = lambda a, nrow: a.reshape(bsz, seq, -1)[:, seq - nrow:]
    new = {'ret': ret_s, 'lru_h': lru_h, 'lru_conv': last(proj_b[:, :LRU_WIDTH], 3), 'ssd': ssd_s,
           'ssd_conv': last(proj_c[:, SSD_INNER:], 3), 'rwkv': rwkv_s, 'rwkv_shift': last(proj_d, 1)}
    return jnp.concatenate([o_a, o_b, o_c, o_d], axis=1), new


def _layer(x, xb, groups, states, p, alpha):
    (bp, lp, pos_p), (bs, ls, pos_s) = groups
    n_p = bp * lp
    w_parts = _split_w_in(p['w_in'])
    x_s = x[n_p:]
    projs_p = [_matmul(xb, w.astype(BF16), n_p) for w in w_parts]
    projs_s = [_matmul_f32(x_s, w) for w in w_parts]
    mix_p, new_p = _mixers(projs_p, states[0], p, bp, lp, pos_p, False)
    mix_s, new_s = _mixers(projs_s, states[1], p, bs, ls, pos_s, True)
    x1, x1b = _out_ln(mix_p, mix_s, p['w_out'], x, p['ln1_g'], p['ln1_b'], alpha)
    pm = dict(p)
    for name in ('exp_w_gate', 'exp_w_up', 'exp_w_down', 'sh_w_gate', 'sh_w_up', 'sh_w_down'):
        pm[name] = p[name].astype(BF16)
    x2, x2b = _moe_ln(x1, x1b, pm, alpha)
    return x2, x2b, (new_p, new_s)


_PARAM_NAMES = ('w_in', 'w_out', 'ret_norm_w', 'lru_conv_w', 'lru_conv_b', 'lru_wa', 'lru_ba', 'lru_wx', 'lru_bx',
                'lru_lambda', 'lru_norm_w', 'ssd_conv_w', 'ssd_conv_b', 'ssd_dt_bias', 'ssd_A_log', 'ssd_D',
                'ssd_norm_w', 'rwkv_mu', 'rwkv_w0', 'rwkv_w2', 'rwkv_a0', 'rwkv_a2', 'rwkv_g2', 'rwkv_k_k',
                'rwkv_k_a', 'rwkv_r_k', 'rwkv_ln_w', 'rwkv_ln_b', 'ln1_g', 'ln1_b', 'router_w', 'router_bias',
                'exp_w_gate', 'exp_w_up', 'exp_w_down', 'sh_w_gate', 'sh_w_up', 'sh_w_down', 'ln2_g', 'ln2_b')
_STATE_NAMES = ('ret', 'lru_h', 'lru_conv', 'ssd', 'ssd_conv', 'rwkv', 'rwkv_shift')


def _forward(x_prompt, x_sample, states_s, params):
    depth = params['w_in'].shape[0]
    bp, lp, d = x_prompt.shape
    bs, ls, _ = x_sample.shape
    alpha = (2 * depth) ** 0.25
    groups = ((bp, lp, 0), (bs, ls, PAST_LEN))
    x = jnp.concatenate([x_prompt.reshape(bp * lp, d), x_sample.reshape(bs * ls, d)], axis=0)
    xb = x.astype(BF16)
    out_p = {n: [] for n in _STATE_NAMES}
    out_s = {n: [] for n in _STATE_NAMES}
    for l in range(depth):
        p = {n: params[n][l] for n in _PARAM_NAMES}
        st_s = {n: states_s[n][l] for n in _STATE_NAMES}
        st_p = {n: jnp.zeros((bp,) + st_s[n].shape[1:], F32) for n in _STATE_NAMES}
        x, xb, (new_p, new_s) = _layer(x, xb, groups, (st_p, st_s), p, alpha)
        for n in _STATE_NAMES:
            out_p[n].append(new_p[n])
            out_s[n].append(new_s[n])
    stk = lambda lst: jnp.stack(lst, axis=0)
    outs = [x[:bp * lp].reshape(bp, lp, d), x[bp * lp:].reshape(bs, ls, d)]
    for n in _STATE_NAMES:
        outs += [stk(out_p[n]), stk(out_s[n])]
    return tuple(outs)


def kernel(x_prompt, x_sample, state_ret, state_lru, cache_lru_conv, state_ssm, cache_ssm_conv, state_rwkv, cache_rwkv_shift, w_in, w_out, ret_norm_w, lru_conv_w, lru_conv_b, lru_wa, lru_ba, lru_wx, lru_bx, lru_lambda, lru_norm_w, ssd_conv_w, ssd_conv_b, ssd_dt_bias, ssd_A_log, ssd_D, ssd_norm_w, rwkv_mu, rwkv_w0, rwkv_w2, rwkv_a0, rwkv_a2, rwkv_g2, rwkv_k_k, rwkv_k_a, rwkv_r_k, rwkv_ln_w, rwkv_ln_b, ln1_g, ln1_b, router_w, router_bias, exp_w_gate, exp_w_up, exp_w_down, sh_w_gate, sh_w_up, sh_w_down, ln2_g, ln2_b):
    states_s = {'ret': state_ret, 'lru_h': state_lru, 'lru_conv': cache_lru_conv, 'ssd': state_ssm,
                'ssd_conv': cache_ssm_conv, 'rwkv': state_rwkv, 'rwkv_shift': cache_rwkv_shift}
    params = dict(zip(_PARAM_NAMES, (
        w_in, w_out, ret_norm_w, lru_conv_w, lru_conv_b, lru_wa, lru_ba, lru_wx, lru_bx, lru_lambda, lru_norm_w,
        ssd_conv_w, ssd_conv_b, ssd_dt_bias, ssd_A_log, ssd_D, ssd_norm_w, rwkv_mu, rwkv_w0, rwkv_w2, rwkv_a0,
        rwkv_a2, rwkv_g2, rwkv_k_k, rwkv_k_a, rwkv_r_k, rwkv_ln_w, rwkv_ln_b, ln1_g, ln1_b, router_w, router_bias,
        exp_w_gate, exp_w_up, exp_w_down, sh_w_gate, sh_w_up, sh_w_down, ln2_g, ln2_b)))
    return _forward(x_prompt, x_sample, states_s, params)
```

```python
import functools
import math

import numpy as np
import jax
import jax.numpy as jnp
from jax import lax
from jax.experimental import pallas as pl
from jax.experimental.pallas import tpu as pltpu

F32 = jnp.float32
BF16 = jnp.bfloat16
I32 = jnp.int32

D_MODEL = 2048
PAST_LEN = 1024
RET_HEADS, RET_DK, RET_DV = 4, 64, 128
LRU_WIDTH, LRU_BLOCKS, LRU_C = 512, 4, 8.0
SSD_INNER, SSD_HEADDIM, SSD_HEADS, SSD_GROUPS, SSD_STATE = 512, 64, 8, 2, 128
RWKV_WIDTH, RWKV_HEADDIM, RWKV_HEADS = 512, 64, 8
RWKV_LN_EPS = 64e-5
N_EXPERTS, TOP_K, N_EXPERT_GROUPS, TOPK_GROUPS = 64, 8, 8, 4
D_EXPERT = 512
ROUTED_SCALE = 2.5
LN_EPS = 1e-5
RMS_EPS = 1e-6
A_QK = RET_HEADS * RET_DK
A_V = RET_HEADS * RET_DV
COLS_A = 2 * A_QK + 2 * A_V
COLS_B = 2 * LRU_WIDTH
SSD_CONV_DIM = SSD_INNER + 2 * SSD_GROUPS * SSD_STATE
COLS_C = SSD_INNER + SSD_CONV_DIM + SSD_HEADS
COLS_D = 3 * RWKV_WIDTH + 64 + 64 + 128

VMEM_LIMIT_BYTES = 56 * 1024 * 1024
LANES = 128
SUBLANES = 8

TOK_TILE = 512
LN_TILE = 256
MOE_ROWS = 512
COMBINE_TILE = 64
ROUTER_TILE = 256
RET_CHUNK = 256
LRU_CHUNK = 256
SSD_CHUNK = 128
RWKV_CHUNK = 64
RWKV_PRE_TILE = 256

HIGHEST = lax.Precision.HIGHEST


def _cp(sem, vmem=VMEM_LIMIT_BYTES):
    return pltpu.CompilerParams(dimension_semantics=sem, vmem_limit_bytes=vmem)


def _sds(shape, dtype):
    return jax.ShapeDtypeStruct(shape, dtype)


def _full(shape):
    nd = len(shape)
    return pl.BlockSpec(shape, lambda *_: (0,) * nd)


def _tile(n, pref, mult=16):
    for t in range(min(pref, n) // mult * mult, 0, -mult):
        if n % t == 0:
            return t
    return n


def _sigmoid(x):
    return jax.nn.sigmoid(x)


def _silu(x):
    return x * jax.nn.sigmoid(x)


def _softplus(x):
    return jnp.maximum(x, 0.0) + jnp.log1p(jnp.exp(-jnp.abs(x)))


class _Dots:
    def __init__(self, precise):
        self.precise = precise

    def _dg(self, a, b, ca, cb):
        if self.precise:
            return lax.dot_general(a, b, (((ca,), (cb,)), ((), ())), preferred_element_type=F32, precision=HIGHEST)
        return lax.dot_general(a.astype(BF16), b.astype(BF16), (((ca,), (cb,)), ((), ())),
                               preferred_element_type=F32)

    def nn(self, a, b):
        return self._dg(a, b, 1, 0)

    def nt(self, a, b):
        return self._dg(a, b, 1, 1)

    def tn(self, a, b):
        return self._dg(a, b, 0, 0)


def _fdot(a, b):
    return jnp.dot(a, b, preferred_element_type=F32, precision=HIGHEST)


def _mm_body(x_ref, w_ref, o_ref):
    o_ref[...] = jnp.dot(x_ref[...], w_ref[...], preferred_element_type=F32)


def _matmul(x, w, rows):
    k = x.shape[1]
    n = w.shape[1]
    tm = _tile(rows, TOK_TILE)
    return pl.pallas_call(
        _mm_body, grid=(rows // tm,),
        in_specs=[pl.BlockSpec((tm, k), lambda i: (i, 0)), _full((k, n))],
        out_specs=pl.BlockSpec((tm, n), lambda i: (i, 0)),
        out_shape=_sds((rows, n), F32), compiler_params=_cp(("parallel",)), name="proj_matmul")(x, w)


def _mm_f32_body(x_ref, w_ref, o_ref):
    o_ref[...] = _fdot(x_ref[...], w_ref[...])


def _matmul_f32(x, w):
    m, k = x.shape
    n = w.shape[1]
    tn = _tile(n, 2 * LANES, LANES)
    return pl.pallas_call(
        _mm_f32_body, grid=(n // tn,),
        in_specs=[_full((m, k)), pl.BlockSpec((k, tn), lambda j: (0, j))],
        out_specs=pl.BlockSpec((m, tn), lambda j: (0, j)),
        out_shape=_sds((m, n), F32), compiler_params=_cp(("parallel",)), name="proj_matmul_f32")(x, w)


def _ret_body(q_ref, k_ref, v_ref, g_ref, cos_ref, sin_ref, dmask_ref, qdec_ref, kdec_ref, cdec_ref,
              nw_ref, s0_ref, o_ref, sout_ref, s_scr, *, precise):
    c = pl.program_id(1)
    dot = _Dots(precise)

    @pl.when(c == 0)
    def _():
        s_scr[...] = s0_ref[0]

    t = q_ref.shape[0]
    lane = lax.broadcasted_iota(I32, (t, LANES), 1)
    first = (lane % RET_DK) < (RET_DK // 2)
    cos = cos_ref[...]
    sin = sin_ref[...]

    def rot(x):
        other = jnp.where(first, pltpu.roll(x, LANES - RET_DK // 2, 1), pltpu.roll(x, RET_DK // 2, 1))
        return x * cos + other * sin

    qs = [rot(q_ref[:, p * LANES:(p + 1) * LANES]) for p in range(2)]
    ks = [rot(k_ref[:, p * LANES:(p + 1) * LANES]) * (RET_DK ** -0.5) for p in range(2)]
    kds = [ks[p] * kdec_ref[p] for p in range(2)]
    for h in range(RET_HEADS):
        lo = (h % 2) * RET_DK
        qh = qs[h // 2][:, lo:lo + RET_DK]
        kh = ks[h // 2][:, lo:lo + RET_DK]
        kdh = kds[h // 2][:, lo:lo + RET_DK]
        vh = v_ref[:, h * RET_DV:(h + 1) * RET_DV]
        s = s_scr[h]
        att = dot.nt(qh, kh) * dmask_ref[h]
        o = dot.nn(att, vh) + dot.nn(qh, s) * qdec_ref[h]
        s_scr[h] = s * cdec_ref[h] + dot.tn(kdh, vh)
        on = o * lax.rsqrt(jnp.mean(o * o, axis=-1, keepdims=True) + RMS_EPS)
        sl = slice(h * RET_DV, (h + 1) * RET_DV)
        o_ref[:, sl] = (on * nw_ref[:, sl] * _silu(g_ref[:, sl])).astype(o_ref.dtype)

    @pl.when(c == pl.num_programs(1) - 1)
    def _():
        sout_ref[0] = s_scr[...]


def _ret_tables(seq, t, pos0):
    half = RET_DK // 2
    freq = 10000.0 ** (-jnp.linspace(0.0, 1.0, half, dtype=F32))
    pos = (pos0 + jnp.arange(seq, dtype=jnp.int32)).astype(F32)
    ang = pos[:, None] * freq
    cos = jnp.cos(ang)
    sin = jnp.sin(ang)
    cos_t = jnp.tile(cos, (1, LANES // half))
    sin_t = jnp.tile(jnp.concatenate([-sin, sin], axis=1), (1, LANES // RET_DK))
    lg = jnp.log1p(-(2.0 ** (-5.0 - jnp.arange(RET_HEADS, dtype=F32))))
    idx = jnp.arange(t, dtype=F32)
    rel = idx[:, None] - idx[None, :]
    causal = rel >= 0
    dmask = jnp.where(causal[None], jnp.exp(jnp.where(causal, rel, 0.0)[None] * lg[:, None, None]), 0.0)
    q_dec = jnp.exp((idx[None, :] + 1.0) * lg[:, None])
    k_dec = jnp.exp((t - 1.0 - idx[None, :]) * lg[:, None])
    c_dec = jnp.exp(t * lg)
    qdec = jnp.broadcast_to(q_dec[:, :, None], (RET_HEADS, t, RET_DV))
    kdec = jnp.repeat(k_dec.T, RET_DK, axis=1).reshape(t, 2, LANES).transpose(1, 0, 2)
    cdec = jnp.broadcast_to(c_dec[:, None, None], (RET_HEADS, 1, RET_DV))
    return cos_t, sin_t, dmask, qdec, kdec, cdec


def _retention(proj_a, s0, norm_w, bsz, seq, pos0, precise):
    t = min(RET_CHUNK, seq)
    nc = seq // t
    cos_t, sin_t, dmask, qdec, kdec, cdec = _ret_tables(seq, t, pos0)

    def rows(w, col):
        return pl.BlockSpec((t, w), lambda b, c: (b * nc + c, col))

    o, s_out = pl.pallas_call(
        functools.partial(_ret_body, precise=precise), grid=(bsz, nc),
        in_specs=[rows(A_QK, 0), rows(A_QK, 1), rows(A_V, 1), rows(A_V, 2),
                  pl.BlockSpec((t, LANES), lambda b, c: (c, 0)),
                  pl.BlockSpec((t, LANES), lambda b, c: (c, 0)),
                  _full((RET_HEADS, t, t)), _full((RET_HEADS, t, RET_DV)), _full((2, t, LANES)),
                  _full((RET_HEADS, 1, RET_DV)), _full((1, A_V)),
                  pl.BlockSpec((1, RET_HEADS, RET_DK, RET_DV), lambda b, c: (b, 0, 0, 0))],
        out_specs=[pl.BlockSpec((t, A_V), lambda b, c: (b * nc + c, 0)),
                   pl.BlockSpec((1, RET_HEADS, RET_DK, RET_DV), lambda b, c: (b, 0, 0, 0))],
        out_shape=[_sds((bsz * seq, A_V), F32 if precise else BF16), _sds((bsz, RET_HEADS, RET_DK, RET_DV), F32)],
        scratch_shapes=[pltpu.VMEM((RET_HEADS, RET_DK, RET_DV), F32)],
        compiler_params=_cp(("parallel", "arbitrary")), name="retention",
    )(proj_a, proj_a, proj_a, proj_a, cos_t, sin_t, dmask, qdec, kdec, cdec, norm_w.reshape(1, A_V), s0)
    return o, s_out


def _conv_chunk(x, xp_scr, cw_ref, cb_ref, t):
    kw = cw_ref.shape[0]
    xp_scr[SUBLANES:SUBLANES + t, :] = x
    y = cb_ref[...] + xp_scr[SUBLANES - kw + 1:SUBLANES - kw + 1 + t, :] * cw_ref[0:1, :]
    for j in range(1, kw):
        off = SUBLANES - kw + 1 + j
        y = y + xp_scr[off:off + t, :] * cw_ref[j:j + 1, :]
    xp_scr[0:SUBLANES, :] = xp_scr[t:t + SUBLANES, :]
    return y


def _pad_conv_state(buf):
    return jnp.pad(buf, ((0, 0), (SUBLANES - buf.shape[1], 0), (0, 0)))


def _lru_body(xb_ref, gb_ref, cbuf_ref, h0_ref, cw_ref, cb_ref, wa_ref, ba_ref, wx_ref, bx_ref, lam_ref,
              nw_ref, o_ref, hout_ref, xp_scr, h_scr, *, first_pos_is_zero, precise):
    c = pl.program_id(1)
    dot = _Dots(precise)
    t = xb_ref.shape[0]

    @pl.when(c == 0)
    def _():
        xp_scr[0:SUBLANES, :] = cbuf_ref[0]
        h_scr[...] = h0_ref[0]

    xc = _conv_chunk(xb_ref[...], xp_scr, cw_ref, cb_ref, t)
    bw = LRU_WIDTH // LRU_BLOCKS
    ra, ri = [], []
    for n in range(LRU_BLOCKS):
        xh = xc[:, n * bw:(n + 1) * bw]
        ra.append(dot.nn(xh, wa_ref[n]))
        ri.append(dot.nn(xh, wx_ref[n]))
    r = _sigmoid(jnp.concatenate(ra, axis=1) + ba_ref[...])
    i = _sigmoid(jnp.concatenate(ri, axis=1) + bx_ref[...])
    log_a = -LRU_C * r * _softplus(-lam_ref[...])
    a = jnp.exp(log_a)
    th = jnp.tanh(log_a)
    mult = jnp.sqrt(-2.0 * th / (1.0 - th))
    row = lax.broadcasted_iota(I32, (t, LRU_WIDTH), 0)
    if first_pos_is_zero:
        mult = jnp.where((row + c * t) == 0, 1.0, mult)
    b = mult * (i * xc)
    s = 1
    while s < t:
        keep = row >= s
        a_sh = jnp.where(keep, pltpu.roll(a, s, 0), 1.0)
        b_sh = jnp.where(keep, pltpu.roll(b, s, 0), 0.0)
        b = a * b_sh + b
        a = a * a_sh
        s *= 2
    h = a * h_scr[...] + b
    h_scr[...] = h[t - 1:t, :]
    gb = gb_ref[...]
    gelu = 0.5 * gb * (1.0 + jnp.tanh(math.sqrt(2.0 / math.pi) * (gb + 0.044715 * (gb * gb * gb))))
    y = h * gelu
    yn = y * lax.rsqrt(jnp.mean(y * y, axis=-1, keepdims=True) + RMS_EPS)
    o_ref[...] = (yn * nw_ref[...]).astype(o_ref.dtype)

    @pl.when(c == pl.num_programs(1) - 1)
    def _():
        hout_ref[0] = h_scr[...]


def _rg_lru(proj_b, cbuf, h0, p, bsz, seq, pos0, precise):
    t = min(LRU_CHUNK, seq)
    nc = seq // t
    w = LRU_WIDTH
    bw = w // LRU_BLOCKS
    row1 = lambda v: v.reshape(1, w)
    o, h_out = pl.pallas_call(
        functools.partial(_lru_body, first_pos_is_zero=(pos0 == 0), precise=precise), grid=(bsz, nc),
        in_specs=[pl.BlockSpec((t, w), lambda b, c: (b * nc + c, 0)),
                  pl.BlockSpec((t, w), lambda b, c: (b * nc + c, 1)),
                  pl.BlockSpec((1, SUBLANES, w), lambda b, c: (b, 0, 0)),
                  pl.BlockSpec((1, 1, w), lambda b, c: (b, 0, 0)),
                  _full((4, w)), _full((1, w)), _full((LRU_BLOCKS, bw, bw)), _full((1, w)),
                  _full((LRU_BLOCKS, bw, bw)), _full((1, w)), _full((1, w)), _full((1, w))],
        out_specs=[pl.BlockSpec((t, w), lambda b, c: (b * nc + c, 0)),
                   pl.BlockSpec((1, 1, w), lambda b, c: (b, 0, 0))],
        out_shape=[_sds((bsz * seq, w), F32 if precise else BF16), _sds((bsz, 1, w), F32)],
        scratch_shapes=[pltpu.VMEM((t + SUBLANES, w), F32), pltpu.VMEM((1, w), F32)],
        compiler_params=_cp(("parallel", "arbitrary")), name="rg_lru",
    )(proj_b, proj_b, _pad_conv_state(cbuf), h0.reshape(bsz, 1, w), p['lru_conv_w'], row1(p['lru_conv_b']),
      p['lru_wa'], row1(p['lru_ba']), p['lru_wx'], row1(p['lru_bx']),
      row1(p['lru_lambda']), row1(p['lru_norm_w']))
    return o, h_out.reshape(bsz, w)


def _ssd_body(z_ref, xs_ref, bc_ref, dt_ref, dtt_ref, cbuf_ref, s0_ref, cw_ref, cb_ref, dtb_ref, dtbc_ref,
              arow_ref, acol_ref, dexp_ref, nw_ref, tril_ref, triu_ref, o_ref, sout_ref, xp_scr, st_scr, *, precise):
    c = pl.program_id(1)
    dot = _Dots(precise)
    t = z_ref.shape[0]
    npair = SSD_HEADS // 2

    @pl.when(c == 0)
    def _():
        xp_scr[0:SUBLANES, :] = cbuf_ref[0]
        for p in range(npair):
            st_scr[p] = s0_ref[0, 2 * p:2 * p + 2].reshape(LANES, SSD_STATE).T

    xin = jnp.concatenate([xs_ref[...], bc_ref[...]], axis=1)
    xbc = _silu(_conv_chunk(xin, xp_scr, cw_ref, cb_ref, t))
    xs = xbc[:, :SSD_INNER]
    gs = SSD_GROUPS * SSD_STATE
    bm = [xbc[:, SSD_INNER + g * SSD_STATE:SSD_INNER + (g + 1) * SSD_STATE] for g in range(SSD_GROUPS)]
    cm = [xbc[:, SSD_INNER + gs + g * SSD_STATE:SSD_INNER + gs + (g + 1) * SSD_STATE] for g in range(SSD_GROUPS)]
    dt_c = _softplus(dt_ref[...] + dtb_ref[...])
    dt_r = _softplus(dtt_ref[0] + dtbc_ref[:, :t])
    cum_c = _fdot(tril_ref[...], dt_c * arow_ref[...])
    cum_r = _fdot(dt_r * acol_ref[:, :t], triu_ref[...])
    cb = [dot.nt(cm[g], bm[g]) for g in range(SSD_GROUPS)]
    rowi = lax.broadcasted_iota(I32, (t, t), 0)
    coli = lax.broadcasted_iota(I32, (t, t), 1)
    tri = rowi >= coli
    lane = lax.broadcasted_iota(I32, (t, LANES), 1)
    lo = lane < SSD_HEADDIM
    ys = []
    for p in range(npair):
        g = (2 * p) // (SSD_HEADS // SSD_GROUPS)
        x_pair = xs[:, p * LANES:(p + 1) * LANES]
        y_pair = jnp.zeros((t, LANES), F32)
        cc, cend, te = [], [], []
        for hh in range(2):
            h = 2 * p + hh
            cch = cum_c[:, h:h + 1]
            seg = cch - cum_r[h:h + 1, :]
            lmat = jnp.where(tri, jnp.exp(jnp.where(tri, seg, 0.0)), 0.0)
            m = cb[g] * lmat * dt_r[h:h + 1, :]
            xm = jnp.where(lo if hh == 0 else jnp.logical_not(lo), x_pair, 0.0)
            y_pair = y_pair + dot.nn(m, xm)
            ce = cum_c[t - 1:t, h:h + 1]
            cc.append(cch)
            cend.append(ce)
            te.append(jnp.exp(ce - cch) * dt_c[:, h:h + 1])
        st = st_scr[p]
        y_pair = y_pair + dot.nn(cm[g], st) * jnp.where(lo, jnp.exp(cc[0]), jnp.exp(cc[1]))
        xt = x_pair * jnp.where(lo, te[0], te[1])
        st_scr[p] = st * jnp.where(lo[0:1, :], jnp.exp(cend[0]), jnp.exp(cend[1])) + dot.tn(bm[g], xt)
        ys.append(y_pair)
    y = jnp.concatenate(ys, axis=1)
    z = z_ref[...]
    y = (y + xs * dexp_ref[...]) * _silu(z)
    gw = SSD_INNER // SSD_GROUPS
    outs = []
    for g in range(SSD_GROUPS):
        yg = y[:, g * gw:(g + 1) * gw]
        outs.append(yg * lax.rsqrt(jnp.mean(yg * yg, axis=-1, keepdims=True) + RMS_EPS))
    o_ref[...] = (jnp.concatenate(outs, axis=1) * nw_ref[...]).astype(o_ref.dtype)

    @pl.when(c == pl.num_programs(1) - 1)
    def _():
        for p in range(npair):
            sout_ref[0, 2 * p:2 * p + 2] = st_scr[p].T.reshape(2, SSD_HEADDIM, SSD_STATE)


def _ssd(proj_c, dt_raw, cbuf, s0, p, bsz, seq, precise):
    t = min(SSD_CHUNK, seq)
    nc = seq // t
    w = SSD_INNER
    cw = SSD_CONV_DIM
    dtt = jnp.swapaxes(dt_raw[:, :SSD_HEADS].reshape(bsz, seq, SSD_HEADS), 1, 2)
    dtb = p['ssd_dt_bias'].astype(F32)
    a = -jnp.exp(p['ssd_A_log'].astype(F32))
    pad8 = lambda v: jnp.pad(v, (0, LANES - SSD_HEADS)).reshape(1, LANES)
    col8 = lambda v: jnp.broadcast_to(v[:, None], (SSD_HEADS, LANES))
    dexp = jnp.repeat(p['ssd_D'].astype(F32), SSD_HEADDIM).reshape(1, w)
    tril = jnp.tril(jnp.ones((t, t), F32))
    o, s_out = pl.pallas_call(
        functools.partial(_ssd_body, precise=precise), grid=(bsz, nc),
        in_specs=[pl.BlockSpec((t, w), lambda b, c: (b * nc + c, 0)),
                  pl.BlockSpec((t, w), lambda b, c: (b * nc + c, 1)),
                  pl.BlockSpec((t, w), lambda b, c: (b * nc + c, 2)),
                  pl.BlockSpec((t, LANES), lambda b, c: (b * nc + c, 0)),
                  pl.BlockSpec((1, SSD_HEADS, t), lambda b, c: (b, 0, c)),
                  pl.BlockSpec((1, SUBLANES, cw), lambda b, c: (b, 0, 0)),
                  pl.BlockSpec((1, SSD_HEADS, SSD_HEADDIM, SSD_STATE), lambda b, c: (b, 0, 0, 0)),
                  _full((4, cw)), _full((1, cw)), _full((1, LANES)), _full((SSD_HEADS, LANES)),
                  _full((1, LANES)), _full((SSD_HEADS, LANES)), _full((1, w)), _full((1, w)),
                  _full((t, t)), _full((t, t))],
        out_specs=[pl.BlockSpec((t, w), lambda b, c: (b * nc + c, 0)),
                   pl.BlockSpec((1, SSD_HEADS, SSD_HEADDIM, SSD_STATE), lambda b, c: (b, 0, 0, 0))],
        out_shape=[_sds((bsz * seq, w), F32 if precise else BF16), _sds((bsz, SSD_HEADS, SSD_HEADDIM, SSD_STATE), F32)],
        scratch_shapes=[pltpu.VMEM((t + SUBLANES, cw), F32), pltpu.VMEM((SSD_HEADS // 2, SSD_STATE, LANES), F32)],
        compiler_params=_cp(("parallel", "arbitrary")), name="ssd",
    )(proj_c, proj_c, proj_c, dt_raw, dtt, _pad_conv_state(cbuf), s0, p['ssd_conv_w'],
      p['ssd_conv_b'].reshape(1, cw), pad8(dtb), col8(dtb), pad8(a), col8(a), dexp,
      p['ssd_norm_w'].reshape(1, w), tril, tril.T)
    return o, s_out


def _rwkv_pre_body(pd_ref, prev_ref, shift_ref, mu_ref, wlora_ref, g2_ref, w0_ref, a0_ref, kk_ref, ka_ref,
                   rk_ref, ones_ref, r_ref, lw_ref, k_ref, v_ref, nkk_ref, bb_ref, bonus_ref, og_ref, *, precise):
    c = pl.program_id(1)
    dot = _Dots(precise)
    t = pd_ref.shape[0]
    w = RWKV_WIDTH
    pd = pd_ref[...]
    prev_row = jnp.where(c == 0, shift_ref[0], prev_ref[SUBLANES - 1:SUBLANES, :])
    row = lax.broadcasted_iota(I32, pd.shape, 0)
    prev = jnp.where(row == 0, prev_row, pltpu.roll(pd, 1, 0))
    mixed = pd + (prev - pd) * mu_ref[...]
    r = mixed[:, 0:w]
    kd = mixed[:, w:2 * w]
    vd = mixed[:, 2 * w:3 * w]
    lora_in = mixed[:, 3 * w:3 * w + LANES]
    lane = lax.broadcasted_iota(I32, lora_in.shape, 1)
    lora_in = jnp.where(lane < 64, jnp.tanh(lora_in), lora_in)
    lora = dot.nn(lora_in, wlora_ref[...])
    gd = mixed[:, 3 * w + LANES:3 * w + 2 * LANES]
    og_ref[...] = dot.nn(_sigmoid(gd), g2_ref[...])
    w_log = -_softplus(-(w0_ref[...] + lora[:, :w])) - 0.5
    lw_ref[...] = -jnp.exp(w_log)
    iclr = _sigmoid(a0_ref[...] + lora[:, w:])
    kk = kd * kk_ref[...]
    ss = _fdot(kk * kk, ones_ref[...])
    kk = kk / jnp.maximum(jnp.sqrt(ss), 1e-12)
    kr = kd * (1.0 + (iclr - 1.0) * ka_ref[...])
    r_ref[...] = r
    k_ref[...] = kr
    v_ref[...] = vd
    nkk_ref[...] = -kk
    bb_ref[...] = kk * iclr
    bonus_ref[...] = _fdot(r * kr * rk_ref[...], ones_ref[...]) * vd


def _head_ones():
    hid = jnp.arange(RWKV_WIDTH) // RWKV_HEADDIM
    return (hid[:, None] == hid[None, :]).astype(F32)


def _rwkv_pre(proj_d, shift, p, bsz, seq, precise):
    t = min(RWKV_PRE_TILE, seq)
    nc = seq // t
    w = RWKV_WIDTH
    tb = t // SUBLANES
    wlora = jnp.zeros((LANES, 2 * w), F32).at[:64, :w].set(p['rwkv_w2']).at[64:, w:].set(p['rwkv_a2'])
    row1 = lambda v: v.reshape(1, -1)
    outs = pl.pallas_call(
        functools.partial(_rwkv_pre_body, precise=precise), grid=(bsz, nc),
        in_specs=[pl.BlockSpec((t, COLS_D), lambda b, c: (b * nc + c, 0)),
                  pl.BlockSpec((SUBLANES, COLS_D), lambda b, c: (jnp.maximum((b * nc + c) * tb - 1, 0), 0)),
                  pl.BlockSpec((1, 1, COLS_D), lambda b, c: (b, 0, 0)),
                  _full((1, COLS_D)), _full((LANES, 2 * w)), _full((LANES, w)), _full((1, w)), _full((1, w)),
                  _full((1, w)), _full((1, w)), _full((1, w)), _full((w, w))],
        out_specs=[pl.BlockSpec((t, w), lambda b, c: (b * nc + c, 0))] * 8,
        out_shape=[_sds((bsz * seq, w), F32)] * 8,
        compiler_params=_cp(("parallel", "parallel")), name="rwkv_pre",
    )(proj_d, proj_d, shift, row1(p['rwkv_mu']), wlora, p['rwkv_g2'], row1(p['rwkv_w0']),
      row1(p['rwkv_a0']), row1(p['rwkv_k_k']), row1(p['rwkv_k_a']), row1(p['rwkv_r_k']), _head_ones())
    return outs


def _rwkv_scan_body(r_ref, lw_ref, k_ref, v_ref, a_ref, b_ref, s0_ref, tril_ref, o_ref, sout_ref, s_scr, *, precise):
    c = pl.program_id(1)
    dot = _Dots(precise)
    hd = RWKV_HEADDIM
    gw = 4 * hd

    @pl.when(c == 0)
    def _():
        s_scr[...] = s0_ref[...]

    row = lax.broadcasted_iota(I32, (hd, gw), 0)
    lane = lax.broadcasted_iota(I32, (hd, gw), 1)
    col = jnp.bitwise_and(lane, hd - 1)
    strict = col < row
    incl = col <= row
    eye = jnp.where(col == row, 1.0, 0.0)
    lane_blk = jnp.right_shift(lane, 6)
    bd_mask = (jnp.right_shift(lax.broadcasted_iota(I32, (gw, gw), 0), 6)
               == jnp.right_shift(lax.broadcasted_iota(I32, (gw, gw), 1), 6))

    def bd(x):
        xo = x if precise else x.astype(BF16)
        return jnp.where(bd_mask, jnp.concatenate([xo] * 4, axis=0), jnp.zeros((), xo.dtype))

    tril = tril_ref[...]
    for bb in range(2):
        for g in range(RWKV_HEADS // 4):
            sl = slice(g * gw, (g + 1) * gw)
            r, lw, k, v = r_ref[bb, :, sl], lw_ref[bb, :, sl], k_ref[bb, :, sl], v_ref[bb, :, sl]
            a, b = a_ref[bb, :, sl], b_ref[bb, :, sl]
            cum = _fdot(tril, lw)
            at = a * jnp.exp(cum - lw)
            rt = r * jnp.exp(cum)
            inv = jnp.exp(-cum)
            bt = b * inv
            kt = k * inv
            to_end = jnp.exp(cum[hd - 1:hd, :] - cum)
            gram = dot.nt(jnp.concatenate([at, rt], axis=0), jnp.concatenate([bd(bt), bd(kt)], axis=0))
            l_ab = jnp.where(strict, gram[:hd, :gw], 0.0)
            l_ak = jnp.where(strict, gram[:hd, gw:], 0.0)
            m_rb = jnp.where(incl, gram[hd:, :gw], 0.0)
            m_rk = jnp.where(incl, gram[hd:, gw:], 0.0)
            t_inv = eye + l_ab
            lp = dot.nn(l_ab, bd(l_ab))
            for level in range(5):
                both = dot.nn(jnp.concatenate([t_inv, lp], axis=0), bd(lp))
                t_inv = t_inv + both[:hd]
                lp = both[hd:]
            bd_v = bd(v)
            z = dot.nn(l_ak, bd_v)
            gu = dot.nn(t_inv, jnp.concatenate([bd(at), bd(z)], axis=1))
            g_mat, u_loc = gu[:, :gw], gu[:, gw:]
            qo = dot.nn(m_rb, jnp.concatenate([bd(g_mat), bd(u_loc)], axis=1))
            q_hat = rt + qo[:, :gw]
            o_loc = qo[:, gw:] + dot.nn(m_rk, bd_v)
            s = s_scr[bb, :, sl]
            su = dot.nt(jnp.concatenate([q_hat, g_mat], axis=0), bd(s))
            o_ref[bb, :, sl] = su[:hd] + o_loc
            u = su[hd:] + u_loc
            full = dot.tn(jnp.concatenate([u, v], axis=0), jnp.concatenate([b * to_end, k * to_end], axis=0))
            s_new = s * jnp.exp(cum[hd - 1:hd, :])
            for h in range(4):
                s_new = s_new + jnp.where(lane_blk == h, full[h * hd:(h + 1) * hd, :], 0.0)
            s_scr[bb, :, sl] = s_new

    @pl.when(c == pl.num_programs(1) - 1)
    def _():
        sout_ref[...] = s_scr[...]


def _rwkv_scan(r, lw, k, v, nkk, bb, s0, bsz, seq, precise):
    wd = RWKV_WIDTH
    hd = RWKV_HEADDIM
    t = RWKV_CHUNK
    pad = (-seq) % t
    nc = (seq + pad) // t
    r3 = lambda x: jnp.pad(x.reshape(bsz, seq, wd), ((0, 0), (0, pad), (0, 0)))
    s0s = s0.transpose(0, 2, 1, 3).reshape(bsz, hd, wd)
    spec = pl.BlockSpec((2, t, wd), lambda b, c: (b, c, 0))
    sspec = pl.BlockSpec((2, hd, wd), lambda b, c: (b, 0, 0))
    o, s_out = pl.pallas_call(
        functools.partial(_rwkv_scan_body, precise=precise), grid=(bsz // 2, nc),
        in_specs=[spec] * 6 + [sspec, _full((t, t))], out_specs=[spec, sspec],
        out_shape=[_sds((bsz, seq + pad, wd), F32), _sds((bsz, hd, wd), F32)],
        scratch_shapes=[pltpu.VMEM((2, hd, wd), F32)],
        compiler_params=_cp(("parallel", "arbitrary")), name="rwkv_scan",
    )(r3(r), r3(lw), r3(k), r3(v), r3(nkk), r3(bb), s0s, jnp.tril(jnp.ones((t, t), F32)))
    s_fin = s_out.reshape(bsz, hd, RWKV_HEADS, hd).transpose(0, 2, 1, 3)
    return o[:, :seq].reshape(bsz * seq, wd), s_fin


def _rwkv_post_body(o_ref, bonus_ref, og_ref, lw_ref, lb_ref, ones_ref, out_ref):
    o = o_ref[...]
    inv = 1.0 / RWKV_HEADDIM
    mean = _fdot(o, ones_ref[...]) * inv
    xc = o - mean
    var = _fdot(xc * xc, ones_ref[...]) * inv
    y = xc * lax.rsqrt(var + RWKV_LN_EPS) * lw_ref[...] + lb_ref[...]
    out_ref[...] = ((y + bonus_ref[...]) * og_ref[...]).astype(out_ref.dtype)


def _rwkv_post(o, bonus, og, p, precise):
    n = o.shape[0]
    t = _tile(n, RWKV_PRE_TILE)
    w = RWKV_WIDTH
    spec = pl.BlockSpec((t, w), lambda i: (i, 0))
    return pl.pallas_call(
        _rwkv_post_body, grid=(n // t,),
        in_specs=[spec, spec, spec, _full((1, w)), _full((1, w)), _full((w, w))],
        out_specs=spec, out_shape=_sds((n, w), F32 if precise else BF16),
        compiler_params=_cp(("parallel",)), name="rwkv_post",
    )(o, bonus, og, p['rwkv_ln_w'].reshape(1, w), p['rwkv_ln_b'].reshape(1, w), _head_ones())


def _layer_norm_rows(z, g, b):
    zc = z - jnp.mean(z, axis=-1, keepdims=True)
    return zc * lax.rsqrt(jnp.mean(zc * zc, axis=-1, keepdims=True) + LN_EPS) * g + b


def _out_ln_body(mixp_ref, mixs_ref, w_hbm, x_ref, g_ref, b_ref, y_ref, yb_ref, wf_scr, wb_scr, sem, *, alpha,
                 prompt_tiles):
    i = pl.program_id(0)

    @pl.when(i == 0)
    def _():
        copy = pltpu.make_async_copy(w_hbm, wf_scr, sem.at[0])
        copy.start()
        copy.wait()
        wb_scr[...] = wf_scr[...].astype(BF16)

    def finish(mix):
        y = _layer_norm_rows(alpha * x_ref[...] + mix, g_ref[...], b_ref[...])
        y_ref[...] = y
        yb_ref[...] = y.astype(BF16)

    @pl.when(i < prompt_tiles)
    def _():
        finish(jnp.dot(mixp_ref[...], wb_scr[...], preferred_element_type=F32))

    @pl.when(i >= prompt_tiles)
    def _():
        finish(_fdot(mixs_ref[...], wf_scr[...]))


def _out_ln(mix_p, mix_s, w_out, x, g, b, alpha):
    n, d = x.shape
    n_p, n_s = mix_p.shape[0], mix_s.shape[0]
    tm = _tile(math.gcd(n_p, n_s), LN_TILE)
    pt = n_p // tm
    rows = lambda: pl.BlockSpec((tm, d), lambda i: (i, 0))
    return pl.pallas_call(
        functools.partial(_out_ln_body, alpha=alpha, prompt_tiles=pt), grid=(n // tm,),
        in_specs=[pl.BlockSpec((tm, d), lambda i: (jnp.minimum(i, pt - 1), 0)),
                  pl.BlockSpec((tm, d), lambda i: (jnp.maximum(i - pt, 0), 0)),
                  pl.BlockSpec(memory_space=pl.ANY), rows(), _full((1, d)), _full((1, d))],
        out_specs=[rows(), rows()], out_shape=[_sds((n, d), F32), _sds((n, d), BF16)],
        scratch_shapes=[pltpu.VMEM((d, d), F32), pltpu.VMEM((d, d), BF16), pltpu.SemaphoreType.DMA((1,))],
        compiler_params=_cp(("arbitrary",)), name="out_proj_ln",
    )(mix_p, mix_s, w_out, x, g.reshape(1, d), b.reshape(1, d))


def _router_body(x_ref, wt_ref, bias_ref, tri_ref, idx_ref, gate_ref, rank_ref, cnt_ref, cnt_scr):
    i = pl.program_id(0)
    tm = x_ref.shape[0]
    ne, ng = N_EXPERTS, N_EXPERT_GROUPS
    per = ne // ng

    @pl.when(i == 0)
    def _():
        cnt_scr[...] = jnp.zeros(cnt_scr.shape, F32)

    logits = lax.dot_general(wt_ref[...], x_ref[...], (((1,), (1,)), ((), ())), preferred_element_type=F32,
                             precision=HIGHEST)
    scores = _sigmoid(logits)
    biased = scores + bias_ref[...]
    neg = -jnp.inf
    b3 = biased.reshape(ng, per, tm)
    e3 = lax.broadcasted_iota(I32, (ng, per, tm), 1)
    m1 = jnp.max(b3, axis=1, keepdims=True)
    i1 = jnp.min(jnp.where(b3 == m1, e3, per), axis=1, keepdims=True)
    m2 = jnp.max(jnp.where(e3 == i1, neg, b3), axis=1, keepdims=True)
    grp = (m1 + m2).reshape(ng, tm)
    gi = lax.broadcasted_iota(I32, (ng, tm), 0)
    keep = jnp.zeros((ng, tm), jnp.bool_)
    for _ in range(TOPK_GROUPS):
        m = jnp.max(grp, axis=0, keepdims=True)
        first = jnp.min(jnp.where(grp == m, gi, ng), axis=0, keepdims=True)
        sel = gi == first
        keep = jnp.logical_or(keep, sel)
        grp = jnp.where(sel, neg, grp)
    keep_e = jnp.broadcast_to(keep.astype(F32).reshape(ng, 1, tm), (ng, per, tm)).reshape(ne, tm) > 0.5
    masked = jnp.where(keep_e, biased, neg)
    ei = lax.broadcasted_iota(I32, (ne, tm), 0)
    sels, idxs, gates = [], [], []
    chosen = jnp.zeros((ne, tm), F32)
    for _ in range(TOP_K):
        m = jnp.max(masked, axis=0, keepdims=True)
        first = jnp.min(jnp.where(masked == m, ei, ne), axis=0, keepdims=True)
        sel = ei == first
        idxs.append(first)
        gates.append(jnp.sum(jnp.where(sel, scores, 0.0), axis=0, keepdims=True))
        sels.append(sel)
        chosen = chosen + sel.astype(F32)
        masked = jnp.where(sel, neg, masked)
    gsum = gates[0]
    for k in range(1, TOP_K):
        gsum = gsum + gates[k]
    before = jnp.dot(chosen.astype(BF16), tri_ref[...], preferred_element_type=F32) + cnt_scr[:, 0:tm]
    for k in range(TOP_K):
        idx_ref[k:k + 1, :] = idxs[k]
        gate_ref[k:k + 1, :] = gates[k] / gsum * ROUTED_SCALE
        rank_ref[k:k + 1, :] = jnp.sum(jnp.where(sels[k], before, 0.0), axis=0, keepdims=True).astype(I32)
    cnt_scr[...] = cnt_scr[...] + jnp.sum(chosen, axis=1, keepdims=True)
    cnt_ref[...] = cnt_scr[...]


def _router(x, router_w, router_bias):
    n, d = x.shape
    tm = _tile(n, ROUTER_TILE, LANES)
    tri = (jnp.arange(tm)[:, None] < jnp.arange(tm)[None, :]).astype(BF16)
    bias = jnp.broadcast_to(router_bias.astype(F32)[:, None], (N_EXPERTS, tm))
    tok = lambda: pl.BlockSpec((TOP_K, tm), lambda i: (0, i))
    return pl.pallas_call(
        _router_body, grid=(n // tm,),
        in_specs=[pl.BlockSpec((tm, d), lambda i: (i, 0)), _full((N_EXPERTS, d)), _full((N_EXPERTS, tm)),
                  _full((tm, tm))],
        out_specs=[tok(), tok(), tok(), _full((N_EXPERTS, tm))],
        out_shape=[_sds((TOP_K, n), I32), _sds((TOP_K, n), F32), _sds((TOP_K, n), I32), _sds((N_EXPERTS, tm), F32)],
        scratch_shapes=[pltpu.VMEM((N_EXPERTS, tm), F32)],
        compiler_params=_cp(("arbitrary",)), name="router",
    )(x, router_w.T, bias, tri)


def _ffn(xb, wg, wu, wd):
    h = _silu(jnp.dot(xb, wg, preferred_element_type=F32)) * jnp.dot(xb, wu, preferred_element_type=F32)
    return jnp.dot(h.astype(BF16), wd, preferred_element_type=F32)


def _shared_body(x_ref, wg_ref, wu_ref, wd_ref, o_ref):
    o_ref[...] = _ffn(x_ref[...], wg_ref[...], wu_ref[...], wd_ref[...])


def _shared_ffn(xb, wg, wu, wd):
    n, d = xb.shape
    tm = _tile(n, TOK_TILE)
    f = wg.shape[1]
    return pl.pallas_call(
        _shared_body, grid=(n // tm,),
        in_specs=[pl.BlockSpec((tm, d), lambda i: (i, 0)), _full((d, f)), _full((d, f)), _full((f, d))],
        out_specs=pl.BlockSpec((tm, d), lambda i: (i, 0)), out_shape=_sds((n, d), F32),
        compiler_params=_cp(("parallel",)), name="shared_ffn")(xb, wg, wu, wd)


def _gather_rows_copy(x_hbm, buf, sem, slot, r, tok):
    return pltpu.make_async_copy(x_hbm.at[pl.ds(tok, 1)], buf.at[slot, pl.ds(r, 1)], sem.at[slot])


def _expert_body(be_ref, nu_ref, tok0_ref, tok1_ref, x_hbm, wg_ref, wu_ref, wd_ref, y_ref, xbuf, sem):
    i = pl.program_id(0)
    n_used = nu_ref[0]
    slot = i % 2
    rows = xbuf.shape[1]

    def issue(tok_ref, dst_slot):
        def one(r, carry):
            _gather_rows_copy(x_hbm, xbuf, sem, dst_slot, r, tok_ref[0, 0, r]).start()
            return carry
        lax.fori_loop(0, rows, one, 0, unroll=8)

    @pl.when(jnp.logical_and(i == 0, n_used > 0))
    def _():
        issue(tok0_ref, 0)

    @pl.when(i + 1 < n_used)
    def _():
        issue(tok1_ref, 1 - slot)

    @pl.when(i < n_used)
    def _():
        pltpu.make_async_copy(x_hbm.at[pl.ds(0, rows)], xbuf.at[slot], sem.at[slot]).wait()
        y_ref[...] = _ffn(xbuf[slot].astype(BF16), wg_ref[...], wu_ref[...], wd_ref[...])

    @pl.when(i >= n_used)
    def _():
        y_ref[...] = jnp.zeros(y_ref.shape, F32)


def _expert_ffn(x, row_tok, block_e, n_used, wg, wu, wd):
    n, d = x.shape
    nb = block_e.shape[0]
    rows = MOE_ROWS
    f = wg.shape[2]
    tok3 = row_tok.reshape(nb, 1, rows)
    smem = pltpu.SMEM
    grid_spec = pltpu.PrefetchScalarGridSpec(
        num_scalar_prefetch=2, grid=(nb,),
        in_specs=[pl.BlockSpec((1, 1, rows), lambda i, be, nu: (i, 0, 0), memory_space=smem),
                  pl.BlockSpec((1, 1, rows), lambda i, be, nu: (jnp.minimum(i + 1, nb - 1), 0, 0), memory_space=smem),
                  pl.BlockSpec(memory_space=pl.ANY),
                  pl.BlockSpec((None, d, f), lambda i, be, nu: (be[i], 0, 0)),
                  pl.BlockSpec((None, d, f), lambda i, be, nu: (be[i], 0, 0)),
                  pl.BlockSpec((None, f, d), lambda i, be, nu: (be[i], 0, 0))],
        out_specs=pl.BlockSpec((rows, d), lambda i, be, nu: (i, 0)),
        scratch_shapes=[pltpu.VMEM((2, rows, d), F32), pltpu.SemaphoreType.DMA((2,))])
    return pl.pallas_call(
        _expert_body, grid_spec=grid_spec, out_shape=_sds((nb * rows, d), F32),
        compiler_params=_cp(("arbitrary",)), name="expert_ffn",
    )(block_e, n_used, tok3, tok3, x, wg, wu, wd)


def _combine_copy(y_hbm, buf, sem, slot, j, row):
    return pltpu.make_async_copy(y_hbm.at[pl.ds(row, 1)], buf.at[slot, pl.ds(j, 1)], sem.at[slot])


def _combine_body(d0_ref, d1_ref, y_hbm, gate_ref, sh_ref, x_ref, g_ref, b_ref, o_ref, ob_ref, ybuf, sem, *, alpha):
    i = pl.program_id(0)
    nt = pl.num_programs(0)
    slot = i % 2
    nrow = ybuf.shape[1]
    tc = x_ref.shape[0]

    def issue(d_ref, dst_slot):
        def one(j, carry):
            _combine_copy(y_hbm, ybuf, sem, dst_slot, j, d_ref[0, 0, j]).start()
            return carry
        lax.fori_loop(0, nrow, one, 0, unroll=8)

    @pl.when(i == 0)
    def _():
        issue(d0_ref, 0)

    @pl.when(i + 1 < nt)
    def _():
        issue(d1_ref, 1 - slot)

    pltpu.make_async_copy(y_hbm.at[pl.ds(0, nrow)], ybuf.at[slot], sem.at[slot]).wait()
    gate = gate_ref[...]
    routed = ybuf[slot, 0:tc, :] * gate[:, 0:1]
    for k in range(1, TOP_K):
        routed = routed + ybuf[slot, k * tc:(k + 1) * tc, :] * gate[:, k:k + 1]
    y = _layer_norm_rows(alpha * x_ref[...] + (routed + sh_ref[...]), g_ref[...], b_ref[...])
    o_ref[...] = y
    ob_ref[...] = y.astype(BF16)


def _combine(y_sorted, dest, gate_t, shared, x, g, b, alpha):
    n, d = x.shape
    tc = _tile(n, COMBINE_TILE, SUBLANES)
    nt = n // tc
    dtile = dest.reshape(TOP_K, nt, tc).transpose(1, 0, 2).reshape(nt, 1, TOP_K * tc)
    rows = lambda: pl.BlockSpec((tc, d), lambda i: (i, 0))
    smem = pltpu.SMEM
    return pl.pallas_call(
        functools.partial(_combine_body, alpha=alpha), grid=(nt,),
        in_specs=[pl.BlockSpec((1, 1, TOP_K * tc), lambda i: (i, 0, 0), memory_space=smem),
                  pl.BlockSpec((1, 1, TOP_K * tc), lambda i: (jnp.minimum(i + 1, nt - 1), 0, 0), memory_space=smem),
                  pl.BlockSpec(memory_space=pl.ANY),
                  pl.BlockSpec((tc, TOP_K), lambda i: (i, 0)), rows(), rows(), _full((1, d)), _full((1, d))],
        out_specs=[rows(), rows()], out_shape=[_sds((n, d), F32), _sds((n, d), BF16)],
        scratch_shapes=[pltpu.VMEM((2, TOP_K * tc, d), F32), pltpu.SemaphoreType.DMA((2,))],
        compiler_params=_cp(("arbitrary",)), name="moe_combine",
    )(dtile, dtile, y_sorted, gate_t, shared, x, g.reshape(1, d), b.reshape(1, d))


def _moe_ln(x, xb, p, alpha):
    n, d = x.shape
    idx, gate, rank, cnt = _router(x, p['router_w'], p['router_bias'])
    counts = cnt[:, 0].astype(I32)
    rows = MOE_ROWS
    padded = (counts + rows - 1) // rows * rows
    pad_end = jnp.cumsum(padded)
    pad_start = pad_end - padded
    dest = pad_start[idx] + rank
    nb = (n * TOP_K) // rows + N_EXPERTS
    n_used = (pad_end[-1] // rows).astype(I32).reshape(1)
    block_start = jnp.arange(nb, dtype=I32) * rows
    block_e = jnp.minimum(jnp.sum((pad_end[None, :] <= block_start[:, None]).astype(I32), axis=1), N_EXPERTS - 1)
    tok = jnp.broadcast_to(jnp.arange(n, dtype=I32)[None, :], (TOP_K, n))
    row_tok = jnp.zeros((nb * rows,), I32).at[dest.reshape(-1)].set(tok.reshape(-1), unique_indices=True)
    y_sorted = _expert_ffn(x, row_tok, block_e, n_used, p['exp_w_gate'], p['exp_w_up'], p['exp_w_down'])
    shared = _shared_ffn(xb, p['sh_w_gate'], p['sh_w_up'], p['sh_w_down'])
    return _combine(y_sorted, dest, gate.T, shared, x, p['ln2_g'], p['ln2_b'], alpha)


def _split_w_in(w_in):
    a0, b0, c0, d0 = 0, COLS_A, COLS_A + COLS_B, COLS_A + COLS_B + COLS_C
    w_dt = jnp.pad(w_in[:, c0 + SSD_INNER + SSD_CONV_DIM:d0], ((0, 0), (0, LANES - SSD_HEADS)))
    return w_in[:, a0:b0], w_in[:, b0:c0], w_in[:, c0:c0 + SSD_INNER + SSD_CONV_DIM], w_dt, w_in[:, d0:]


def _mixers(projs, st, p, bsz, seq, pos0, precise):
    proj_a, proj_b, proj_c, dt_raw, proj_d = projs
    o_a, ret_s = _retention(proj_a, st['ret'], p['ret_norm_w'], bsz, seq, pos0, precise)
    o_b, lru_h = _rg_lru(proj_b, st['lru_conv'], st['lru_h'], p, bsz, seq, pos0, precise)
    o_c, ssd_s = _ssd(proj_c, dt_raw, st['ssd_conv'], st['ssd'], p, bsz, seq, precise)
    r, lw, k, v, nkk, bb, bonus, og = _rwkv_pre(proj_d, st['rwkv_shift'], p, bsz, seq, precise)
    o, rwkv_s = _rwkv_scan(r, lw, k, v, nkk, bb, st['rwkv'], bsz, seq, precise)
    o_d = _rwkv_post(o, bonus, og, p, precise)
    last = lambda a, nrow: a.reshape(bsz, seq, -1)[:, seq - nrow:]
    new = {'ret': ret_s, 'lru_h': lru_h, 'lru_conv': last(proj_b[:, :LRU_WIDTH], 3), 'ssd': ssd_s,
           'ssd_conv': last(proj_c[:, SSD_INNER:], 3), 'rwkv': rwkv_s, 'rwkv_shift': last(proj_d, 1)}
    return jnp.concatenate([o_a, o_b, o_c, o_d], axis=1), new


def _layer(x, xb, groups, states, p, alpha):
    (bp, lp, pos_p), (bs, ls, pos_s) = groups
    n_p = bp * lp
    w_parts = _split_w_in(p['w_in'])
    x_s = x[n_p:]
    projs_p = [_matmul(xb, w.astype(BF16), n_p) for w in w_parts]
    projs_s = [_matmul_f32(x_s, w) for w in w_parts]
    mix_p, new_p = _mixers(projs_p, states[0], p, bp, lp, pos_p, False)
    mix_s, new_s = _mixers(projs_s, states[1], p, bs, ls, pos_s, True)
    x1, x1b = _out_ln(mix_p, mix_s, p['w_out'], x, p['ln1_g'], p['ln1_b'], alpha)
    pm = dict(p)
    for name in ('exp_w_gate', 'exp_w_up', 'exp_w_down', 'sh_w_gate', 'sh_w_up', 'sh_w_down'):
        pm[name] = p[name].astype(BF16)
    x2, x2b = _moe_ln(x1, x1b, pm, alpha)
    return x2, x2b, (new_p, new_s)


_PARAM_NAMES = ('w_in', 'w_out', 'ret_norm_w', 'lru_conv_w', 'lru_conv_b', 'lru_wa', 'lru_ba', 'lru_wx', 'lru_bx',
                'lru_lambda', 'lru_norm_w', 'ssd_conv_w', 'ssd_conv_b', 'ssd_dt_bias', 'ssd_A_log', 'ssd_D',
                'ssd_norm_w', 'rwkv_mu', 'rwkv_w0', 'rwkv_w2', 'rwkv_a0', 'rwkv_a2', 'rwkv_g2', 'rwkv_k_k',
                'rwkv_k_a', 'rwkv_r_k', 'rwkv_ln_w', 'rwkv_ln_b', 'ln1_g', 'ln1_b', 'router_w', 'router_bias',
                'exp_w_gate', 'exp_w_up', 'exp_w_down', 'sh_w_gate', 'sh_w_up', 'sh_w_down', 'ln2_g', 'ln2_b')
_STATE_NAMES = ('ret', 'lru_h', 'lru_conv', 'ssd', 'ssd_conv', 'rwkv', 'rwkv_shift')


def _forward(x_prompt, x_sample, states_s, params):
    depth = params['w_in'].shape[0]
    bp, lp, d = x_prompt.shape
    bs, ls, _ = x_sample.shape
    alpha = (2 * depth) ** 0.25
    groups = ((bp, lp, 0), (bs, ls, PAST_LEN))
    x = jnp.concatenate([x_prompt.reshape(bp * lp, d), x_sample.reshape(bs * ls, d)], axis=0)
    xb = x.astype(BF16)
    out_p = {n: [] for n in _STATE_NAMES}
    out_s = {n: [] for n in _STATE_NAMES}
    for l in range(depth):
        p = {n: params[n][l] for n in _PARAM_NAMES}
        st_s = {n: states_s[n][l] for n in _STATE_NAMES}
        st_p = {n: jnp.zeros((bp,) + st_s[n].shape[1:], F32) for n in _STATE_NAMES}
        x, xb, (new_p, new_s) = _layer(x, xb, groups, (st_p, st_s), p, alpha)
        for n in _STATE_NAMES:
            out_p[n].append(new_p[n])
            out_s[n].append(new_s[n])
    stk = lambda lst: jnp.stack(lst, axis=0)
    outs = [x[:bp * lp].reshape(bp, lp, d), x[bp * lp:].reshape(bs, ls, d)]
    for n in _STATE_NAMES:
        outs += [stk(out_p[n]), stk(out_s[n])]
    return tuple(outs)


def kernel(x_prompt, x_sample, state_ret, state_lru, cache_lru_conv, state_ssm, cache_ssm_conv, state_rwkv, cache_rwkv_shift, w_in, w_out, ret_norm_w, lru_conv_w, lru_conv_b, lru_wa, lru_ba, lru_wx, lru_bx, lru_lambda, lru_norm_w, ssd_conv_w, ssd_conv_b, ssd_dt_bias, ssd_A_log, ssd_D, ssd_norm_w, rwkv_mu, rwkv_w0, rwkv_w2, rwkv_a0, rwkv_a2, rwkv_g2, rwkv_k_k, rwkv_k_a, rwkv_r_k, rwkv_ln_w, rwkv_ln_b, ln1_g, ln1_b, router_w, router_bias, exp_w_gate, exp_w_up, exp_w_down, sh_w_gate, sh_w_up, sh_w_down, ln2_g, ln2_b):
    states_s = {'ret': state_ret, 'lru_h': state_lru, 'lru_conv': cache_lru_conv, 'ssd': state_ssm,
                'ssd_conv': cache_ssm_conv, 'rwkv': state_rwkv, 'rwkv_shift': cache_rwkv_shift}
    params = dict(zip(_PARAM_NAMES, (
        w_in, w_out, ret_norm_w, lru_conv_w, lru_conv_b, lru_wa, lru_ba, lru_wx, lru_bx, lru_lambda, lru_norm_w,
        ssd_conv_w, ssd_conv_b, ssd_dt_bias, ssd_A_log, ssd_D, ssd_norm_w, rwkv_mu, rwkv_w0, rwkv_w2, rwkv_a0,
        rwkv_a2, rwkv_g2, rwkv_k_k, rwkv_k_a, rwkv_r_k, rwkv_ln_w, rwkv_ln_b, ln1_g, ln1_b, router_w, router_bias,
        exp_w_gate, exp_w_up, exp_w_down, sh_w_gate, sh_w_up, sh_w_down, ln2_g, ln2_b)))
    return _forward(x_prompt, x_sample, states_s, params)
```

```python
import functools
import math

import numpy as np
import jax
import jax.numpy as jnp
from jax import lax
from jax.experimental import pallas as pl
from jax.experimental.pallas import tpu as pltpu

F32 = jnp.float32
BF16 = jnp.bfloat16
I32 = jnp.int32

D_MODEL = 2048
PAST_LEN = 1024
RET_HEADS, RET_DK, RET_DV = 4, 64, 128
LRU_WIDTH, LRU_BLOCKS, LRU_C = 512, 4, 8.0
SSD_INNER, SSD_HEADDIM, SSD_HEADS, SSD_GROUPS, SSD_STATE = 512, 64, 8, 2, 128
RWKV_WIDTH, RWKV_HEADDIM, RWKV_HEADS = 512, 64, 8
RWKV_LN_EPS = 64e-5
N_EXPERTS, TOP_K, N_EXPERT_GROUPS, TOPK_GROUPS = 64, 8, 8, 4
D_EXPERT = 512
ROUTED_SCALE = 2.5
LN_EPS = 1e-5
RMS_EPS = 1e-6
A_QK = RET_HEADS * RET_DK
A_V = RET_HEADS * RET_DV
COLS_A = 2 * A_QK + 2 * A_V
COLS_B = 2 * LRU_WIDTH
SSD_CONV_DIM = SSD_INNER + 2 * SSD_GROUPS * SSD_STATE
COLS_C = SSD_INNER + SSD_CONV_DIM + SSD_HEADS
COLS_D = 3 * RWKV_WIDTH + 64 + 64 + 128

VMEM_LIMIT_BYTES = 56 * 1024 * 1024
LANES = 128
SUBLANES = 8

TOK_TILE = 512
LN_TILE = 256
MOE_ROWS = 512
COMBINE_TILE = 64
DISPATCH_TILE = 256
DEST_TILE = 4096
ROUTER_TILE = 256
RET_CHUNK = 256
LRU_CHUNK = 256
SSD_CHUNK = 128
RWKV_CHUNK = 64
RWKV_PRE_TILE = 256

HIGHEST = lax.Precision.HIGHEST


def _cp(sem, vmem=VMEM_LIMIT_BYTES):
    return pltpu.CompilerParams(dimension_semantics=sem, vmem_limit_bytes=vmem)


def _sds(shape, dtype):
    return jax.ShapeDtypeStruct(shape, dtype)


def _full(shape):
    nd = len(shape)
    return pl.BlockSpec(shape, lambda *_: (0,) * nd)


def _tile(n, pref, mult=16):
    for t in range(min(pref, n) // mult * mult, 0, -mult):
        if n % t == 0:
            return t
    return n


def _sigmoid(x):
    return jax.nn.sigmoid(x)


def _silu(x):
    return x * jax.nn.sigmoid(x)


def _softplus(x):
    return jnp.maximum(x, 0.0) + jnp.log1p(jnp.exp(-jnp.abs(x)))


class _Dots:
    def __init__(self, precise):
        self.precise = precise

    def _dg(self, a, b, ca, cb):
        if self.precise:
            return lax.dot_general(a, b, (((ca,), (cb,)), ((), ())), preferred_element_type=F32, precision=HIGHEST)
        return lax.dot_general(a.astype(BF16), b.astype(BF16), (((ca,), (cb,)), ((), ())),
                               preferred_element_type=F32)

    def nn(self, a, b):
        return self._dg(a, b, 1, 0)

    def nt(self, a, b):
        return self._dg(a, b, 1, 1)

    def tn(self, a, b):
        return self._dg(a, b, 0, 0)


def _fdot(a, b):
    return jnp.dot(a, b, preferred_element_type=F32, precision=HIGHEST)


def _mm_body(x_ref, w_ref, o_ref):
    o_ref[...] = jnp.dot(x_ref[...], w_ref[...], preferred_element_type=F32)


def _matmul(x, w, rows):
    k = x.shape[1]
    n = w.shape[1]
    tm = _tile(rows, TOK_TILE)
    return pl.pallas_call(
        _mm_body, grid=(rows // tm,),
        in_specs=[pl.BlockSpec((tm, k), lambda i: (i, 0)), _full((k, n))],
        out_specs=pl.BlockSpec((tm, n), lambda i: (i, 0)),
        out_shape=_sds((rows, n), F32), compiler_params=_cp(("parallel",)), name="proj_matmul")(x, w)


def _mm_f32_body(x_ref, w_ref, o_ref):
    o_ref[...] = _fdot(x_ref[...], w_ref[...])


def _matmul_f32(x, w):
    m, k = x.shape
    n = w.shape[1]
    tn = _tile(n, 2 * LANES, LANES)
    return pl.pallas_call(
        _mm_f32_body, grid=(n // tn,),
        in_specs=[_full((m, k)), pl.BlockSpec((k, tn), lambda j: (0, j))],
        out_specs=pl.BlockSpec((m, tn), lambda j: (0, j)),
        out_shape=_sds((m, n), F32), compiler_params=_cp(("parallel",)), name="proj_matmul_f32")(x, w)


def _ret_body(q_ref, k_ref, v_ref, g_ref, cos_ref, sin_ref, dmask_ref, qdec_ref, kdec_ref, cdec_ref,
              nw_ref, s0_ref, o_ref, sout_ref, s_scr, *, precise):
    c = pl.program_id(1)
    dot = _Dots(precise)

    @pl.when(c == 0)
    def _():
        s_scr[...] = s0_ref[0]

    t = q_ref.shape[0]
    lane = lax.broadcasted_iota(I32, (t, LANES), 1)
    first = (lane % RET_DK) < (RET_DK // 2)
    cos = cos_ref[...]
    sin = sin_ref[...]

    def rot(x):
        other = jnp.where(first, pltpu.roll(x, LANES - RET_DK // 2, 1), pltpu.roll(x, RET_DK // 2, 1))
        return x * cos + other * sin

    qs = [rot(q_ref[:, p * LANES:(p + 1) * LANES]) for p in range(2)]
    ks = [rot(k_ref[:, p * LANES:(p + 1) * LANES]) * (RET_DK ** -0.5) for p in range(2)]
    kds = [ks[p] * kdec_ref[p] for p in range(2)]
    for h in range(RET_HEADS):
        lo = (h % 2) * RET_DK
        qh = qs[h // 2][:, lo:lo + RET_DK]
        kh = ks[h // 2][:, lo:lo + RET_DK]
        kdh = kds[h // 2][:, lo:lo + RET_DK]
        vh = v_ref[:, h * RET_DV:(h + 1) * RET_DV]
        s = s_scr[h]
        att = dot.nt(qh, kh) * dmask_ref[h]
        o = dot.nn(att, vh) + dot.nn(qh, s) * qdec_ref[h]
        s_scr[h] = s * cdec_ref[h] + dot.tn(kdh, vh)
        on = o * lax.rsqrt(jnp.mean(o * o, axis=-1, keepdims=True) + RMS_EPS)
        sl = slice(h * RET_DV, (h + 1) * RET_DV)
        o_ref[:, sl] = (on * nw_ref[:, sl] * _silu(g_ref[:, sl])).astype(o_ref.dtype)

    @pl.when(c == pl.num_programs(1) - 1)
    def _():
        sout_ref[0] = s_scr[...]


def _ret_tables(seq, t, pos0):
    half = RET_DK // 2
    freq = 10000.0 ** (-jnp.linspace(0.0, 1.0, half, dtype=F32))
    pos = (pos0 + jnp.arange(seq, dtype=jnp.int32)).astype(F32)
    ang = pos[:, None] * freq
    cos = jnp.cos(ang)
    sin = jnp.sin(ang)
    cos_t = jnp.tile(cos, (1, LANES // half))
    sin_t = jnp.tile(jnp.concatenate([-sin, sin], axis=1), (1, LANES // RET_DK))
    lg = jnp.log1p(-(2.0 ** (-5.0 - jnp.arange(RET_HEADS, dtype=F32))))
    idx = jnp.arange(t, dtype=F32)
    rel = idx[:, None] - idx[None, :]
    causal = rel >= 0
    dmask = jnp.where(causal[None], jnp.exp(jnp.where(causal, rel, 0.0)[None] * lg[:, None, None]), 0.0)
    q_dec = jnp.exp((idx[None, :] + 1.0) * lg[:, None])
    k_dec = jnp.exp((t - 1.0 - idx[None, :]) * lg[:, None])
    c_dec = jnp.exp(t * lg)
    qdec = jnp.broadcast_to(q_dec[:, :, None], (RET_HEADS, t, RET_DV))
    kdec = jnp.repeat(k_dec.T, RET_DK, axis=1).reshape(t, 2, LANES).transpose(1, 0, 2)
    cdec = jnp.broadcast_to(c_dec[:, None, None], (RET_HEADS, 1, RET_DV))
    return cos_t, sin_t, dmask, qdec, kdec, cdec


def _retention(proj_a, s0, norm_w, bsz, seq, pos0, precise):
    t = min(RET_CHUNK, seq)
    nc = seq // t
    cos_t, sin_t, dmask, qdec, kdec, cdec = _ret_tables(seq, t, pos0)

    def rows(w, col):
        return pl.BlockSpec((t, w), lambda b, c: (b * nc + c, col))

    o, s_out = pl.pallas_call(
        functools.partial(_ret_body, precise=precise), grid=(bsz, nc),
        in_specs=[rows(A_QK, 0), rows(A_QK, 1), rows(A_V, 1), rows(A_V, 2),
                  pl.BlockSpec((t, LANES), lambda b, c: (c, 0)),
                  pl.BlockSpec((t, LANES), lambda b, c: (c, 0)),
                  _full((RET_HEADS, t, t)), _full((RET_HEADS, t, RET_DV)), _full((2, t, LANES)),
                  _full((RET_HEADS, 1, RET_DV)), _full((1, A_V)),
                  pl.BlockSpec((1, RET_HEADS, RET_DK, RET_DV), lambda b, c: (b, 0, 0, 0))],
        out_specs=[pl.BlockSpec((t, A_V), lambda b, c: (b * nc + c, 0)),
                   pl.BlockSpec((1, RET_HEADS, RET_DK, RET_DV), lambda b, c: (b, 0, 0, 0))],
        out_shape=[_sds((bsz * seq, A_V), F32 if precise else BF16), _sds((bsz, RET_HEADS, RET_DK, RET_DV), F32)],
        scratch_shapes=[pltpu.VMEM((RET_HEADS, RET_DK, RET_DV), F32)],
        compiler_params=_cp(("parallel", "arbitrary")), name="retention",
    )(proj_a, proj_a, proj_a, proj_a, cos_t, sin_t, dmask, qdec, kdec, cdec, norm_w.reshape(1, A_V), s0)
    return o, s_out


def _conv_chunk(x, xp_scr, cw_ref, cb_ref, t):
    kw = cw_ref.shape[0]
    xp_scr[SUBLANES:SUBLANES + t, :] = x
    y = cb_ref[...] + xp_scr[SUBLANES - kw + 1:SUBLANES - kw + 1 + t, :] * cw_ref[0:1, :]
    for j in range(1, kw):
        off = SUBLANES - kw + 1 + j
        y = y + xp_scr[off:off + t, :] * cw_ref[j:j + 1, :]
    xp_scr[0:SUBLANES, :] = xp_scr[t:t + SUBLANES, :]
    return y


def _pad_conv_state(buf):
    return jnp.pad(buf, ((0, 0), (SUBLANES - buf.shape[1], 0), (0, 0)))


def _lru_body(xb_ref, gb_ref, cbuf_ref, h0_ref, cw_ref, cb_ref, wa_ref, ba_ref, wx_ref, bx_ref, lam_ref,
              nw_ref, o_ref, hout_ref, xp_scr, h_scr, *, first_pos_is_zero, precise):
    c = pl.program_id(1)
    dot = _Dots(precise)
    t = xb_ref.shape[0]

    @pl.when(c == 0)
    def _():
        xp_scr[0:SUBLANES, :] = cbuf_ref[0]
        h_scr[...] = h0_ref[0]

    xc = _conv_chunk(xb_ref[...], xp_scr, cw_ref, cb_ref, t)
    bw = LRU_WIDTH // LRU_BLOCKS
    ra, ri = [], []
    for n in range(LRU_BLOCKS):
        xh = xc[:, n * bw:(n + 1) * bw]
        ra.append(dot.nn(xh, wa_ref[n]))
        ri.append(dot.nn(xh, wx_ref[n]))
    r = _sigmoid(jnp.concatenate(ra, axis=1) + ba_ref[...])
    i = _sigmoid(jnp.concatenate(ri, axis=1) + bx_ref[...])
    log_a = -LRU_C * r * _softplus(-lam_ref[...])
    a = jnp.exp(log_a)
    th = jnp.tanh(log_a)
    mult = jnp.sqrt(-2.0 * th / (1.0 - th))
    row = lax.broadcasted_iota(I32, (t, LRU_WIDTH), 0)
    if first_pos_is_zero:
        mult = jnp.where((row + c * t) == 0, 1.0, mult)
    b = mult * (i * xc)
    s = 1
    while s < t:
        keep = row >= s
        a_sh = jnp.where(keep, pltpu.roll(a, s, 0), 1.0)
        b_sh = jnp.where(keep, pltpu.roll(b, s, 0), 0.0)
        b = a * b_sh + b
        a = a * a_sh
        s *= 2
    h = a * h_scr[...] + b
    h_scr[...] = h[t - 1:t, :]
    gb = gb_ref[...]
    gelu = 0.5 * gb * (1.0 + jnp.tanh(math.sqrt(2.0 / math.pi) * (gb + 0.044715 * (gb * gb * gb))))
    y = h * gelu
    yn = y * lax.rsqrt(jnp.mean(y * y, axis=-1, keepdims=True) + RMS_EPS)
    o_ref[...] = (yn * nw_ref[...]).astype(o_ref.dtype)

    @pl.when(c == pl.num_programs(1) - 1)
    def _():
        hout_ref[0] = h_scr[...]


def _rg_lru(proj_b, cbuf, h0, p, bsz, seq, pos0, precise):
    t = min(LRU_CHUNK, seq)
    nc = seq // t
    w = LRU_WIDTH
    bw = w // LRU_BLOCKS
    row1 = lambda v: v.reshape(1, w)
    o, h_out = pl.pallas_call(
        functools.partial(_lru_body, first_pos_is_zero=(pos0 == 0), precise=precise), grid=(bsz, nc),
        in_specs=[pl.BlockSpec((t, w), lambda b, c: (b * nc + c, 0)),
                  pl.BlockSpec((t, w), lambda b, c: (b * nc + c, 1)),
                  pl.BlockSpec((1, SUBLANES, w), lambda b, c: (b, 0, 0)),
                  pl.BlockSpec((1, 1, w), lambda b, c: (b, 0, 0)),
                  _full((4, w)), _full((1, w)), _full((LRU_BLOCKS, bw, bw)), _full((1, w)),
                  _full((LRU_BLOCKS, bw, bw)), _full((1, w)), _full((1, w)), _full((1, w))],
        out_specs=[pl.BlockSpec((t, w), lambda b, c: (b * nc + c, 0)),
                   pl.BlockSpec((1, 1, w), lambda b, c: (b, 0, 0))],
        out_shape=[_sds((bsz * seq, w), F32 if precise else BF16), _sds((bsz, 1, w), F32)],
        scratch_shapes=[pltpu.VMEM((t + SUBLANES, w), F32), pltpu.VMEM((1, w), F32)],
        compiler_params=_cp(("parallel", "arbitrary")), name="rg_lru",
    )(proj_b, proj_b, _pad_conv_state(cbuf), h0.reshape(bsz, 1, w), p['lru_conv_w'], row1(p['lru_conv_b']),
      p['lru_wa'], row1(p['lru_ba']), p['lru_wx'], row1(p['lru_bx']),
      row1(p['lru_lambda']), row1(p['lru_norm_w']))
    return o, h_out.reshape(bsz, w)


def _ssd_body(z_ref, xs_ref, bc_ref, dt_ref, dtt_ref, cbuf_ref, s0_ref, cw_ref, cb_ref, dtb_ref, dtbc_ref,
              arow_ref, acol_ref, dexp_ref, nw_ref, tril_ref, triu_ref, o_ref, sout_ref, xp_scr, st_scr, *, precise):
    c = pl.program_id(1)
    dot = _Dots(precise)
    t = z_ref.shape[0]
    npair = SSD_HEADS // 2

    @pl.when(c == 0)
    def _():
        xp_scr[0:SUBLANES, :] = cbuf_ref[0]
        for p in range(npair):
            st_scr[p] = s0_ref[0, 2 * p:2 * p + 2].reshape(LANES, SSD_STATE).T

    xin = jnp.concatenate([xs_ref[...], bc_ref[...]], axis=1)
    xbc = _silu(_conv_chunk(xin, xp_scr, cw_ref, cb_ref, t))
    xs = xbc[:, :SSD_INNER]
    gs = SSD_GROUPS * SSD_STATE
    bm = [xbc[:, SSD_INNER + g * SSD_STATE:SSD_INNER + (g + 1) * SSD_STATE] for g in range(SSD_GROUPS)]
    cm = [xbc[:, SSD_INNER + gs + g * SSD_STATE:SSD_INNER + gs + (g + 1) * SSD_STATE] for g in range(SSD_GROUPS)]
    dt_c = _softplus(dt_ref[...] + dtb_ref[...])
    dt_r = _softplus(dtt_ref[0] + dtbc_ref[:, :t])
    cum_c = _fdot(tril_ref[...], dt_c * arow_ref[...])
    cum_r = _fdot(dt_r * acol_ref[:, :t], triu_ref[...])
    cb = [dot.nt(cm[g], bm[g]) for g in range(SSD_GROUPS)]
    rowi = lax.broadcasted_iota(I32, (t, t), 0)
    coli = lax.broadcasted_iota(I32, (t, t), 1)
    tri = rowi >= coli
    lane = lax.broadcasted_iota(I32, (t, LANES), 1)
    lo = lane < SSD_HEADDIM
    ys = []
    for p in range(npair):
        g = (2 * p) // (SSD_HEADS // SSD_GROUPS)
        x_pair = xs[:, p * LANES:(p + 1) * LANES]
        y_pair = jnp.zeros((t, LANES), F32)
        cc, cend, te = [], [], []
        for hh in range(2):
            h = 2 * p + hh
            cch = cum_c[:, h:h + 1]
            seg = cch - cum_r[h:h + 1, :]
            lmat = jnp.where(tri, jnp.exp(jnp.where(tri, seg, 0.0)), 0.0)
            m = cb[g] * lmat * dt_r[h:h + 1, :]
            xm = jnp.where(lo if hh == 0 else jnp.logical_not(lo), x_pair, 0.0)
            y_pair = y_pair + dot.nn(m, xm)
            ce = cum_c[t - 1:t, h:h + 1]
            cc.append(cch)
            cend.append(ce)
            te.append(jnp.exp(ce - cch) * dt_c[:, h:h + 1])
        st = st_scr[p]
        y_pair = y_pair + dot.nn(cm[g], st) * jnp.where(lo, jnp.exp(cc[0]), jnp.exp(cc[1]))
        xt = x_pair * jnp.where(lo, te[0], te[1])
        st_scr[p] = st * jnp.where(lo[0:1, :], jnp.exp(cend[0]), jnp.exp(cend[1])) + dot.tn(bm[g], xt)
        ys.append(y_pair)
    y = jnp.concatenate(ys, axis=1)
    z = z_ref[...]
    y = (y + xs * dexp_ref[...]) * _silu(z)
    gw = SSD_INNER // SSD_GROUPS
    outs = []
    for g in range(SSD_GROUPS):
        yg = y[:, g * gw:(g + 1) * gw]
        outs.append(yg * lax.rsqrt(jnp.mean(yg * yg, axis=-1, keepdims=True) + RMS_EPS))
    o_ref[...] = (jnp.concatenate(outs, axis=1) * nw_ref[...]).astype(o_ref.dtype)

    @pl.when(c == pl.num_programs(1) - 1)
    def _():
        for p in range(npair):
            sout_ref[0, 2 * p:2 * p + 2] = st_scr[p].T.reshape(2, SSD_HEADDIM, SSD_STATE)


def _ssd(proj_c, dt_raw, cbuf, s0, p, bsz, seq, precise):
    t = min(SSD_CHUNK, seq)
    nc = seq // t
    w = SSD_INNER
    cw = SSD_CONV_DIM
    dtt = jnp.swapaxes(dt_raw[:, :SSD_HEADS].reshape(bsz, seq, SSD_HEADS), 1, 2)
    dtb = p['ssd_dt_bias'].astype(F32)
    a = -jnp.exp(p['ssd_A_log'].astype(F32))
    pad8 = lambda v: jnp.pad(v, (0, LANES - SSD_HEADS)).reshape(1, LANES)
    col8 = lambda v: jnp.broadcast_to(v[:, None], (SSD_HEADS, LANES))
    dexp = jnp.repeat(p['ssd_D'].astype(F32), SSD_HEADDIM).reshape(1, w)
    tril = jnp.tril(jnp.ones((t, t), F32))
    o, s_out = pl.pallas_call(
        functools.partial(_ssd_body, precise=precise), grid=(bsz, nc),
        in_specs=[pl.BlockSpec((t, w), lambda b, c: (b * nc + c, 0)),
                  pl.BlockSpec((t, w), lambda b, c: (b * nc + c, 1)),
                  pl.BlockSpec((t, w), lambda b, c: (b * nc + c, 2)),
                  pl.BlockSpec((t, LANES), lambda b, c: (b * nc + c, 0)),
                  pl.BlockSpec((1, SSD_HEADS, t), lambda b, c: (b, 0, c)),
                  pl.BlockSpec((1, SUBLANES, cw), lambda b, c: (b, 0, 0)),
                  pl.BlockSpec((1, SSD_HEADS, SSD_HEADDIM, SSD_STATE), lambda b, c: (b, 0, 0, 0)),
                  _full((4, cw)), _full((1, cw)), _full((1, LANES)), _full((SSD_HEADS, LANES)),
                  _full((1, LANES)), _full((SSD_HEADS, LANES)), _full((1, w)), _full((1, w)),
                  _full((t, t)), _full((t, t))],
        out_specs=[pl.BlockSpec((t, w), lambda b, c: (b * nc + c, 0)),
                   pl.BlockSpec((1, SSD_HEADS, SSD_HEADDIM, SSD_STATE), lambda b, c: (b, 0, 0, 0))],
        out_shape=[_sds((bsz * seq, w), F32 if precise else BF16), _sds((bsz, SSD_HEADS, SSD_HEADDIM, SSD_STATE), F32)],
        scratch_shapes=[pltpu.VMEM((t + SUBLANES, cw), F32), pltpu.VMEM((SSD_HEADS // 2, SSD_STATE, LANES), F32)],
        compiler_params=_cp(("parallel", "arbitrary")), name="ssd",
    )(proj_c, proj_c, proj_c, dt_raw, dtt, _pad_conv_state(cbuf), s0, p['ssd_conv_w'],
      p['ssd_conv_b'].reshape(1, cw), pad8(dtb), col8(dtb), pad8(a), col8(a), dexp,
      p['ssd_norm_w'].reshape(1, w), tril, tril.T)
    return o, s_out


def _rwkv_pre_body(pd_ref, prev_ref, shift_ref, mu_ref, wlora_ref, g2_ref, w0_ref, a0_ref, kk_ref, ka_ref,
                   rk_ref, ones_ref, r_ref, lw_ref, k_ref, v_ref, nkk_ref, bb_ref, bonus_ref, og_ref, *, precise):
    c = pl.program_id(1)
    dot = _Dots(precise)
    t = pd_ref.shape[0]
    w = RWKV_WIDTH
    pd = pd_ref[...]
    prev_row = jnp.where(c == 0, shift_ref[0], prev_ref[SUBLANES - 1:SUBLANES, :])
    row = lax.broadcasted_iota(I32, pd.shape, 0)
    prev = jnp.where(row == 0, prev_row, pltpu.roll(pd, 1, 0))
    mixed = pd + (prev - pd) * mu_ref[...]
    r = mixed[:, 0:w]
    kd = mixed[:, w:2 * w]
    vd = mixed[:, 2 * w:3 * w]
    lora_in = mixed[:, 3 * w:3 * w + LANES]
    lane = lax.broadcasted_iota(I32, lora_in.shape, 1)
    lora_in = jnp.where(lane < 64, jnp.tanh(lora_in), lora_in)
    lora = dot.nn(lora_in, wlora_ref[...])
    gd = mixed[:, 3 * w + LANES:3 * w + 2 * LANES]
    og_ref[...] = dot.nn(_sigmoid(gd), g2_ref[...])
    w_log = -_softplus(-(w0_ref[...] + lora[:, :w])) - 0.5
    lw_ref[...] = -jnp.exp(w_log)
    iclr = _sigmoid(a0_ref[...] + lora[:, w:])
    kk = kd * kk_ref[...]
    ss = _fdot(kk * kk, ones_ref[...])
    kk = kk / jnp.maximum(jnp.sqrt(ss), 1e-12)
    kr = kd * (1.0 + (iclr - 1.0) * ka_ref[...])
    r_ref[...] = r
    k_ref[...] = kr
    v_ref[...] = vd
    nkk_ref[...] = -kk
    bb_ref[...] = kk * iclr
    bonus_ref[...] = _fdot(r * kr * rk_ref[...], ones_ref[...]) * vd


def _head_ones():
    hid = jnp.arange(RWKV_WIDTH) // RWKV_HEADDIM
    return (hid[:, None] == hid[None, :]).astype(F32)


def _rwkv_pre(proj_d, shift, p, bsz, seq, precise):
    t = min(RWKV_PRE_TILE, seq)
    nc = seq // t
    w = RWKV_WIDTH
    tb = t // SUBLANES
    wlora = jnp.zeros((LANES, 2 * w), F32).at[:64, :w].set(p['rwkv_w2']).at[64:, w:].set(p['rwkv_a2'])
    row1 = lambda v: v.reshape(1, -1)
    outs = pl.pallas_call(
        functools.partial(_rwkv_pre_body, precise=precise), grid=(bsz, nc),
        in_specs=[pl.BlockSpec((t, COLS_D), lambda b, c: (b * nc + c, 0)),
                  pl.BlockSpec((SUBLANES, COLS_D), lambda b, c: (jnp.maximum((b * nc + c) * tb - 1, 0), 0)),
                  pl.BlockSpec((1, 1, COLS_D), lambda b, c: (b, 0, 0)),
                  _full((1, COLS_D)), _full((LANES, 2 * w)), _full((LANES, w)), _full((1, w)), _full((1, w)),
                  _full((1, w)), _full((1, w)), _full((1, w)), _full((w, w))],
        out_specs=[pl.BlockSpec((t, w), lambda b, c: (b * nc + c, 0))] * 8,
        out_shape=[_sds((bsz * seq, w), F32)] * 8,
        compiler_params=_cp(("parallel", "parallel")), name="rwkv_pre",
    )(proj_d, proj_d, shift, row1(p['rwkv_mu']), wlora, p['rwkv_g2'], row1(p['rwkv_w0']),
      row1(p['rwkv_a0']), row1(p['rwkv_k_k']), row1(p['rwkv_k_a']), row1(p['rwkv_r_k']), _head_ones())
    return outs


def _rwkv_scan_body(r_ref, lw_ref, k_ref, v_ref, a_ref, b_ref, s0_ref, tril_ref, o_ref, sout_ref, s_scr, *, precise):
    c = pl.program_id(1)
    dot = _Dots(precise)
    hd = RWKV_HEADDIM
    gw = 4 * hd

    @pl.when(c == 0)
    def _():
        s_scr[...] = s0_ref[...]

    row = lax.broadcasted_iota(I32, (hd, gw), 0)
    lane = lax.broadcasted_iota(I32, (hd, gw), 1)
    col = jnp.bitwise_and(lane, hd - 1)
    strict = col < row
    incl = col <= row
    eye = jnp.where(col == row, 1.0, 0.0)
    lane_blk = jnp.right_shift(lane, 6)
    bd_mask = (jnp.right_shift(lax.broadcasted_iota(I32, (gw, gw), 0), 6)
               == jnp.right_shift(lax.broadcasted_iota(I32, (gw, gw), 1), 6))

    def bd(x):
        xo = x if precise else x.astype(BF16)
        return jnp.where(bd_mask, jnp.concatenate([xo] * 4, axis=0), jnp.zeros((), xo.dtype))

    tril = tril_ref[...]
    for bb in range(2):
        for g in range(RWKV_HEADS // 4):
            sl = slice(g * gw, (g + 1) * gw)
            r, lw, k, v = r_ref[bb, :, sl], lw_ref[bb, :, sl], k_ref[bb, :, sl], v_ref[bb, :, sl]
            a, b = a_ref[bb, :, sl], b_ref[bb, :, sl]
            cum = _fdot(tril, lw)
            at = a * jnp.exp(cum - lw)
            rt = r * jnp.exp(cum)
            inv = jnp.exp(-cum)
            bt = b * inv
            kt = k * inv
            to_end = jnp.exp(cum[hd - 1:hd, :] - cum)
            gram = dot.nt(jnp.concatenate([at, rt], axis=0), jnp.concatenate([bd(bt), bd(kt)], axis=0))
            l_ab = jnp.where(strict, gram[:hd, :gw], 0.0)
            l_ak = jnp.where(strict, gram[:hd, gw:], 0.0)
            m_rb = jnp.where(incl, gram[hd:, :gw], 0.0)
            m_rk = jnp.where(incl, gram[hd:, gw:], 0.0)
            t_inv = eye + l_ab
            lp = dot.nn(l_ab, bd(l_ab))
            for level in range(5):
                both = dot.nn(jnp.concatenate([t_inv, lp], axis=0), bd(lp))
                t_inv = t_inv + both[:hd]
                lp = both[hd:]
            bd_v = bd(v)
            z = dot.nn(l_ak, bd_v)
            gu = dot.nn(t_inv, jnp.concatenate([bd(at), bd(z)], axis=1))
            g_mat, u_loc = gu[:, :gw], gu[:, gw:]
            qo = dot.nn(m_rb, jnp.concatenate([bd(g_mat), bd(u_loc)], axis=1))
            q_hat = rt + qo[:, :gw]
            o_loc = qo[:, gw:] + dot.nn(m_rk, bd_v)
            s = s_scr[bb, :, sl]
            su = dot.nt(jnp.concatenate([q_hat, g_mat], axis=0), bd(s))
            o_ref[bb, :, sl] = su[:hd] + o_loc
            u = su[hd:] + u_loc
            full = dot.tn(jnp.concatenate([u, v], axis=0), jnp.concatenate([b * to_end, k * to_end], axis=0))
            s_new = s * jnp.exp(cum[hd - 1:hd, :])
            for h in range(4):
                s_new = s_new + jnp.where(lane_blk == h, full[h * hd:(h + 1) * hd, :], 0.0)
            s_scr[bb, :, sl] = s_new

    @pl.when(c == pl.num_programs(1) - 1)
    def _():
        sout_ref[...] = s_scr[...]


def _rwkv_scan(r, lw, k, v, nkk, bb, s0, bsz, seq, precise):
    wd = RWKV_WIDTH
    hd = RWKV_HEADDIM
    t = RWKV_CHUNK
    pad = (-seq) % t
    nc = (seq + pad) // t
    r3 = lambda x: jnp.pad(x.reshape(bsz, seq, wd), ((0, 0), (0, pad), (0, 0)))
    s0s = s0.transpose(0, 2, 1, 3).reshape(bsz, hd, wd)
    spec = pl.BlockSpec((2, t, wd), lambda b, c: (b, c, 0))
    sspec = pl.BlockSpec((2, hd, wd), lambda b, c: (b, 0, 0))
    o, s_out = pl.pallas_call(
        functools.partial(_rwkv_scan_body, precise=precise), grid=(bsz // 2, nc),
        in_specs=[spec] * 6 + [sspec, _full((t, t))], out_specs=[spec, sspec],
        out_shape=[_sds((bsz, seq + pad, wd), F32), _sds((bsz, hd, wd), F32)],
        scratch_shapes=[pltpu.VMEM((2, hd, wd), F32)],
        compiler_params=_cp(("parallel", "arbitrary")), name="rwkv_scan",
    )(r3(r), r3(lw), r3(k), r3(v), r3(nkk), r3(bb), s0s, jnp.tril(jnp.ones((t, t), F32)))
    s_fin = s_out.reshape(bsz, hd, RWKV_HEADS, hd).transpose(0, 2, 1, 3)
    return o[:, :seq].reshape(bsz * seq, wd), s_fin


def _rwkv_post_body(o_ref, bonus_ref, og_ref, lw_ref, lb_ref, ones_ref, out_ref):
    o = o_ref[...]
    inv = 1.0 / RWKV_HEADDIM
    mean = _fdot(o, ones_ref[...]) * inv
    xc = o - mean
    var = _fdot(xc * xc, ones_ref[...]) * inv
    y = xc * lax.rsqrt(var + RWKV_LN_EPS) * lw_ref[...] + lb_ref[...]
    out_ref[...] = ((y + bonus_ref[...]) * og_ref[...]).astype(out_ref.dtype)


def _rwkv_post(o, bonus, og, p, precise):
    n = o.shape[0]
    t = _tile(n, RWKV_PRE_TILE)
    w = RWKV_WIDTH
    spec = pl.BlockSpec((t, w), lambda i: (i, 0))
    return pl.pallas_call(
        _rwkv_post_body, grid=(n // t,),
        in_specs=[spec, spec, spec, _full((1, w)), _full((1, w)), _full((w, w))],
        out_specs=spec, out_shape=_sds((n, w), F32 if precise else BF16),
        compiler_params=_cp(("parallel",)), name="rwkv_post",
    )(o, bonus, og, p['rwkv_ln_w'].reshape(1, w), p['rwkv_ln_b'].reshape(1, w), _head_ones())


def _layer_norm_rows(z, g, b):
    zc = z - jnp.mean(z, axis=-1, keepdims=True)
    return zc * lax.rsqrt(jnp.mean(zc * zc, axis=-1, keepdims=True) + LN_EPS) * g + b


def _out_ln_body(*refs, alpha, prompt_tiles, parts):
    mixp_refs, mixs_refs = refs[:parts], refs[parts:2 * parts]
    w_hbm, x_ref, g_ref, b_ref, y_ref, yb_ref, wf_scr, wb_scr, sem = refs[2 * parts:]
    i = pl.program_id(0)
    width = mixp_refs[0].shape[1]

    @pl.when(i == 0)
    def _():
        copy = pltpu.make_async_copy(w_hbm, wf_scr, sem.at[0])
        copy.start()
        copy.wait()
        wb_scr[...] = wf_scr[...].astype(BF16)

    def finish(mix):
        y = _layer_norm_rows(alpha * x_ref[...] + mix, g_ref[...], b_ref[...])
        y_ref[...] = y
        yb_ref[...] = y.astype(BF16)

    def project(mix_refs, w_scr, dot):
        acc = dot(mix_refs[0][...], w_scr[0:width, :])
        for j in range(1, parts):
            acc = acc + dot(mix_refs[j][...], w_scr[j * width:(j + 1) * width, :])
        return acc

    @pl.when(i < prompt_tiles)
    def _():
        finish(project(mixp_refs, wb_scr, lambda a, b: jnp.dot(a, b, preferred_element_type=F32)))

    @pl.when(i >= prompt_tiles)
    def _():
        finish(project(mixs_refs, wf_scr, _fdot))


def _out_ln(mix_p, mix_s, w_out, x, g, b, alpha):
    n, d = x.shape
    parts = len(mix_p)
    n_p, n_s = mix_p[0].shape[0], mix_s[0].shape[0]
    width = mix_p[0].shape[1]
    tm = _tile(math.gcd(n_p, n_s), LN_TILE)
    pt = n_p // tm
    rows = lambda: pl.BlockSpec((tm, d), lambda i: (i, 0))
    spec_p = pl.BlockSpec((tm, width), lambda i: (jnp.minimum(i, pt - 1), 0))
    spec_s = pl.BlockSpec((tm, width), lambda i: (jnp.maximum(i - pt, 0), 0))
    return pl.pallas_call(
        functools.partial(_out_ln_body, alpha=alpha, prompt_tiles=pt, parts=parts), grid=(n // tm,),
        in_specs=[spec_p] * parts + [spec_s] * parts + [pl.BlockSpec(memory_space=pl.ANY), rows(), _full((1, d)),
                                                      _full((1, d))],
        out_specs=[rows(), rows()], out_shape=[_sds((n, d), F32), _sds((n, d), BF16)],
        scratch_shapes=[pltpu.VMEM((d, d), F32), pltpu.VMEM((d, d), BF16), pltpu.SemaphoreType.DMA((1,))],
        compiler_params=_cp(("arbitrary",)), name="out_proj_ln",
    )(*mix_p, *mix_s, w_out, x, g.reshape(1, d), b.reshape(1, d))


def _router_body(x_ref, wt_ref, bias_ref, tri_ref, idx_ref, gate_ref, rank_ref, cnt_ref, cnt_scr):
    i = pl.program_id(0)
    tm = x_ref.shape[0]
    ne, ng = N_EXPERTS, N_EXPERT_GROUPS
    per = ne // ng

    @pl.when(i == 0)
    def _():
        cnt_scr[...] = jnp.zeros(cnt_scr.shape, F32)

    logits = lax.dot_general(wt_ref[...], x_ref[...], (((1,), (1,)), ((), ())), preferred_element_type=F32,
                             precision=HIGHEST)
    scores = _sigmoid(logits)
    biased = scores + bias_ref[...]
    neg = -jnp.inf
    b3 = biased.reshape(ng, per, tm)
    e3 = lax.broadcasted_iota(I32, (ng, per, tm), 1)
    m1 = jnp.max(b3, axis=1, keepdims=True)
    i1 = jnp.min(jnp.where(b3 == m1, e3, per), axis=1, keepdims=True)
    m2 = jnp.max(jnp.where(e3 == i1, neg, b3), axis=1, keepdims=True)
    grp = (m1 + m2).reshape(ng, tm)
    gi = lax.broadcasted_iota(I32, (ng, tm), 0)
    keep = jnp.zeros((ng, tm), jnp.bool_)
    for _ in range(TOPK_GROUPS):
        m = jnp.max(grp, axis=0, keepdims=True)
        first = jnp.min(jnp.where(grp == m, gi, ng), axis=0, keepdims=True)
        sel = gi == first
        keep = jnp.logical_or(keep, sel)
        grp = jnp.where(sel, neg, grp)
    keep_e = jnp.broadcast_to(keep.astype(F32).reshape(ng, 1, tm), (ng, per, tm)).reshape(ne, tm) > 0.5
    masked = jnp.where(keep_e, biased, neg)
    ei = lax.broadcasted_iota(I32, (ne, tm), 0)
    sels, idxs, gates = [], [], []
    chosen = jnp.zeros((ne, tm), F32)
    for _ in range(TOP_K):
        m = jnp.max(masked, axis=0, keepdims=True)
        first = jnp.min(jnp.where(masked == m, ei, ne), axis=0, keepdims=True)
        sel = ei == first
        idxs.append(first)
        gates.append(jnp.sum(jnp.where(sel, scores, 0.0), axis=0, keepdims=True))
        sels.append(sel)
        chosen = chosen + sel.astype(F32)
        masked = jnp.where(sel, neg, masked)
    gsum = gates[0]
    for k in range(1, TOP_K):
        gsum = gsum + gates[k]
    before = jnp.dot(chosen.astype(BF16), tri_ref[...], preferred_element_type=F32) + cnt_scr[:, 0:tm]
    for k in range(TOP_K):
        idx_ref[k:k + 1, :] = idxs[k]
        gate_ref[k:k + 1, :] = gates[k] / gsum * ROUTED_SCALE
        rank_ref[k:k + 1, :] = jnp.sum(jnp.where(sels[k], before, 0.0), axis=0, keepdims=True).astype(I32)
    cnt_scr[...] = cnt_scr[...] + jnp.sum(chosen, axis=1, keepdims=True)
    cnt_ref[...] = cnt_scr[...]


def _router(x, router_w, router_bias):
    n, d = x.shape
    tm = _tile(n, ROUTER_TILE, LANES)
    tri = (jnp.arange(tm)[:, None] < jnp.arange(tm)[None, :]).astype(BF16)
    bias = jnp.broadcast_to(router_bias.astype(F32)[:, None], (N_EXPERTS, tm))
    tok = lambda: pl.BlockSpec((TOP_K, tm), lambda i: (0, i))
    return pl.pallas_call(
        _router_body, grid=(n // tm,),
        in_specs=[pl.BlockSpec((tm, d), lambda i: (i, 0)), _full((N_EXPERTS, d)), _full((N_EXPERTS, tm)),
                  _full((tm, tm))],
        out_specs=[tok(), tok(), tok(), _full((N_EXPERTS, tm))],
        out_shape=[_sds((TOP_K, n), I32), _sds((TOP_K, n), F32), _sds((TOP_K, n), I32), _sds((N_EXPERTS, tm), F32)],
        scratch_shapes=[pltpu.VMEM((N_EXPERTS, tm), F32)],
        compiler_params=_cp(("arbitrary",)), name="router",
    )(x, router_w.T, bias, tri)


def _ffn(xb, wg, wu, wd):
    h = _silu(jnp.dot(xb, wg, preferred_element_type=F32)) * jnp.dot(xb, wu, preferred_element_type=F32)
    return jnp.dot(h.astype(BF16), wd, preferred_element_type=F32)


def _shared_body(x_ref, wg_ref, wu_ref, wd_ref, o_ref):
    o_ref[...] = _ffn(x_ref[...], wg_ref[...], wu_ref[...], wd_ref[...])


def _shared_ffn(xb, wg, wu, wd):
    n, d = xb.shape
    tm = _tile(n, TOK_TILE)
    f = wg.shape[1]
    return pl.pallas_call(
        _shared_body, grid=(n // tm,),
        in_specs=[pl.BlockSpec((tm, d), lambda i: (i, 0)), _full((d, f)), _full((d, f)), _full((f, d))],
        out_specs=pl.BlockSpec((tm, d), lambda i: (i, 0)), out_shape=_sds((n, d), F32),
        compiler_params=_cp(("parallel",)), name="shared_ffn")(xb, wg, wu, wd)


def _dest_body(ps_ref, idx_ref, rank_ref, dest_ref):
    idx = idx_ref[...]
    base = lax.fori_loop(0, N_EXPERTS, lambda e, acc: jnp.where(idx == e, ps_ref[e], acc), jnp.zeros(idx.shape, I32))
    dest_ref[...] = base + rank_ref[...]


def _dest_rows(idx, rank, pad_start):
    n = idx.shape[1]
    tn = _tile(n, DEST_TILE, LANES)
    spec = lambda: pl.BlockSpec((TOP_K, tn), lambda i, ps: (0, i))
    grid_spec = pltpu.PrefetchScalarGridSpec(num_scalar_prefetch=1, grid=(n // tn,), in_specs=[spec(), spec()],
                                             out_specs=spec())
    return pl.pallas_call(_dest_body, grid_spec=grid_spec, out_shape=_sds((TOP_K, n), I32),
                          compiler_params=_cp(("parallel",)), name="moe_dest")(pad_start, idx, rank)


def _dispatch_body(ps_ref, pe_ref, nu_ref, dest_ref, x_hbm, zero_hbm, xs_hbm, sem, zsem, *, tile, rows, nb):
    i = pl.program_id(0)
    slot = i % 2

    def zero_block(first_row):
        dst = xs_hbm.at[pl.ds(pl.multiple_of(first_row, rows), rows)]
        return pltpu.make_async_copy(zero_hbm, dst, zsem.at[0])

    def for_zero_blocks(act):
        def per_expert(e, carry):
            @pl.when(pe_ref[e] > ps_ref[e])
            def _():
                act(zero_block(pe_ref[e] - rows))
            return carry

        def per_block(blk, carry):
            @pl.when(blk >= nu_ref[0])
            def _():
                act(zero_block(blk * rows))
            return carry

        lax.fori_loop(0, N_EXPERTS, per_expert, 0)
        lax.fori_loop(0, nb, per_block, 0)

    @pl.when(i == 0)
    def _():
        for_zero_blocks(lambda cp: cp.start())
        for_zero_blocks(lambda cp: cp.wait())

    for k in range(TOP_K):
        def one(t, carry):
            row = dest_ref[0, 0, k * tile + t]
            pltpu.make_async_copy(x_hbm.at[pl.ds(i * tile + t, 1)], xs_hbm.at[pl.ds(row, 1)], sem.at[slot]).start()
            return carry
        lax.fori_loop(0, tile, one, 0, unroll=8)

    def wait_step(s):
        pltpu.make_async_copy(xs_hbm.at[pl.ds(0, TOP_K * tile)], xs_hbm.at[pl.ds(0, TOP_K * tile)], sem.at[s]).wait()

    @pl.when(i > 0)
    def _():
        wait_step(1 - slot)

    @pl.when(i == pl.num_programs(0) - 1)
    def _():
        wait_step(slot)


def _tile_major(a, tile):
    n = a.shape[1]
    return a.reshape(TOP_K, n // tile, tile).transpose(1, 0, 2).reshape(n // tile, 1, TOP_K * tile)


def _dispatch(x, dest, pad_start, pad_end, n_used, nb):
    n, d = x.shape
    rows = MOE_ROWS
    tile = _tile(n, DISPATCH_TILE, SUBLANES)
    smem = pltpu.SMEM
    tiled = lambda: pl.BlockSpec((1, 1, TOP_K * tile), lambda i, ps, pe, nu: (i, 0, 0), memory_space=smem)
    grid_spec = pltpu.PrefetchScalarGridSpec(
        num_scalar_prefetch=3, grid=(n // tile,),
        in_specs=[tiled(), pl.BlockSpec(memory_space=pl.ANY), pl.BlockSpec(memory_space=pl.ANY)],
        out_specs=pl.BlockSpec(memory_space=pl.ANY),
        scratch_shapes=[pltpu.SemaphoreType.DMA((2,)), pltpu.SemaphoreType.DMA((1,))])
    return pl.pallas_call(
        functools.partial(_dispatch_body, tile=tile, rows=rows, nb=nb), grid_spec=grid_spec,
        out_shape=_sds((nb * rows, d), F32), compiler_params=_cp(("arbitrary",)), name="moe_dispatch",
    )(pad_start, pad_end, n_used, _tile_major(dest, tile), x, jnp.zeros((rows, d), F32))


def _expert_body(be_ref, nu_ref, x_ref, wg_ref, wu_ref, wd_ref, y_ref, wg_b, wu_b, wd_b):
    i = pl.program_id(0)
    n_used = nu_ref[0]

    @pl.when(i < n_used)
    def _():
        @pl.when(jnp.logical_or(i == 0, be_ref[i] != be_ref[jnp.maximum(i - 1, 0)]))
        def _():
            wg_b[...] = wg_ref[...].astype(BF16)
            wu_b[...] = wu_ref[...].astype(BF16)
            wd_b[...] = wd_ref[...].astype(BF16)

        y_ref[...] = _ffn(x_ref[...].astype(BF16), wg_b[...], wu_b[...], wd_b[...])

    @pl.when(i >= n_used)
    def _():
        y_ref[...] = jnp.zeros(y_ref.shape, F32)


def _expert_ffn(xs, block_e, n_used, wg, wu, wd):
    d = xs.shape[1]
    nb = block_e.shape[0]
    rows = MOE_ROWS
    f = wg.shape[2]
    grid_spec = pltpu.PrefetchScalarGridSpec(
        num_scalar_prefetch=2, grid=(nb,),
        in_specs=[pl.BlockSpec((rows, d), lambda i, be, nu: (i, 0)),
                  pl.BlockSpec((None, d, f), lambda i, be, nu: (be[i], 0, 0)),
                  pl.BlockSpec((None, d, f), lambda i, be, nu: (be[i], 0, 0)),
                  pl.BlockSpec((None, f, d), lambda i, be, nu: (be[i], 0, 0))],
        out_specs=pl.BlockSpec((rows, d), lambda i, be, nu: (i, 0)),
        scratch_shapes=[pltpu.VMEM((d, f), BF16), pltpu.VMEM((d, f), BF16), pltpu.VMEM((f, d), BF16)])
    return pl.pallas_call(
        _expert_body, grid_spec=grid_spec, out_shape=_sds((nb * rows, d), F32),
        compiler_params=_cp(("arbitrary",)), name="expert_ffn",
    )(block_e, n_used, xs, wg, wu, wd)


def _combine_body(d0_ref, d1_ref, y_hbm, gate_ref, sh_ref, x_ref, g_ref, b_ref, *rest, alpha, split_tiles):
    out_refs, (ybuf, sem) = rest[:-2], rest[-2:]
    i = pl.program_id(0)
    nt = pl.num_programs(0)
    slot = i % 2
    nrow = ybuf.shape[1]
    tc = x_ref.shape[0]

    def issue(dest_ref, dst_slot):
        def one(j, carry):
            row = dest_ref[0, 0, j]
            pltpu.make_async_copy(y_hbm.at[pl.ds(row, 1)], ybuf.at[dst_slot, pl.ds(j, 1)], sem.at[dst_slot]).start()
            return carry
        lax.fori_loop(0, nrow, one, 0, unroll=8)

    @pl.when(i == 0)
    def _():
        issue(d0_ref, 0)

    @pl.when(i + 1 < nt)
    def _():
        issue(d1_ref, 1 - slot)

    pltpu.make_async_copy(y_hbm.at[pl.ds(0, nrow)], ybuf.at[slot], sem.at[slot]).wait()
    gate = gate_ref[...]
    routed = ybuf[slot, 0:tc, :] * gate[:, 0:1]
    for k in range(1, TOP_K):
        routed = routed + ybuf[slot, k * tc:(k + 1) * tc, :] * gate[:, k:k + 1]
    y = _layer_norm_rows(alpha * x_ref[...] + (routed + sh_ref[...]), g_ref[...], b_ref[...])
    if split_tiles is None:
        out_refs[0][...] = y
        out_refs[1][...] = y.astype(BF16)
    else:
        @pl.when(i < split_tiles)
        def _():
            out_refs[0][...] = y

        @pl.when(i >= split_tiles)
        def _():
            out_refs[1][...] = y


def _combine(y_sorted, dest, gate_t, shared, x, g, b, alpha, split_rows):
    n, d = x.shape
    tc = _tile(n if split_rows is None else math.gcd(split_rows, n - split_rows), COMBINE_TILE, SUBLANES)
    nt = n // tc
    smem = pltpu.SMEM
    rows = lambda: pl.BlockSpec((tc, d), lambda i: (i, 0))
    if split_rows is None:
        st = None
        out_specs = [rows(), rows()]
        out_shape = [_sds((n, d), F32), _sds((n, d), BF16)]
    else:
        st = split_rows // tc
        out_specs = [pl.BlockSpec((tc, d), lambda i: (jnp.minimum(i, st - 1), 0)),
                     pl.BlockSpec((tc, d), lambda i: (jnp.maximum(i - st, 0), 0))]
        out_shape = [_sds((split_rows, d), F32), _sds((n - split_rows, d), F32)]
    dest_t = _tile_major(dest, tc)
    return pl.pallas_call(
        functools.partial(_combine_body, alpha=alpha, split_tiles=st), grid=(nt,),
        in_specs=[pl.BlockSpec((1, 1, TOP_K * tc), lambda i: (i, 0, 0), memory_space=smem),
                  pl.BlockSpec((1, 1, TOP_K * tc), lambda i: (jnp.minimum(i + 1, nt - 1), 0, 0), memory_space=smem),
                  pl.BlockSpec(memory_space=pl.ANY), pl.BlockSpec((tc, TOP_K), lambda i: (i, 0)), rows(), rows(),
                  _full((1, d)), _full((1, d))],
        out_specs=out_specs, out_shape=out_shape,
        scratch_shapes=[pltpu.VMEM((2, TOP_K * tc, d), F32), pltpu.SemaphoreType.DMA((2,))],
        compiler_params=_cp(("arbitrary",)), name="moe_combine",
    )(dest_t, dest_t, y_sorted, gate_t, shared, x, g.reshape(1, d), b.reshape(1, d))


def _moe_ln(x, xb, p, alpha, split_rows):
    n, d = x.shape
    idx, gate, rank, cnt = _router(x, p['router_w'], p['router_bias'])
    counts = cnt[:, 0].astype(I32)
    rows = MOE_ROWS
    padded = (counts + rows - 1) // rows * rows
    pad_end = jnp.cumsum(padded)
    pad_start = pad_end - padded
    nb = (n * TOP_K) // rows + N_EXPERTS
    n_used = (pad_end[-1] // rows).astype(I32).reshape(1)
    block_start = jnp.arange(nb, dtype=I32) * rows
    block_e = jnp.minimum(jnp.sum((pad_end[None, :] <= block_start[:, None]).astype(I32), axis=1), N_EXPERTS - 1)
    dest = _dest_rows(idx, rank, pad_start)
    xs = _dispatch(x, dest, pad_start, pad_end, n_used, nb)
    y_sorted = _expert_ffn(xs, block_e, n_used, p['exp_w_gate'], p['exp_w_up'], p['exp_w_down'])
    shared = _shared_ffn(xb, p['sh_w_gate'].astype(BF16), p['sh_w_up'].astype(BF16), p['sh_w_down'].astype(BF16))
    return _combine(y_sorted, dest, gate.T, shared, x, p['ln2_g'], p['ln2_b'], alpha, split_rows)


def _split_w_in(w_in):
    a0, b0, c0, d0 = 0, COLS_A, COLS_A + COLS_B, COLS_A + COLS_B + COLS_C
    w_dt = jnp.pad(w_in[:, c0 + SSD_INNER + SSD_CONV_DIM:d0], ((0, 0), (0, LANES - SSD_HEADS)))
    return w_in[:, a0:b0], w_in[:, b0:c0], w_in[:, c0:c0 + SSD_INNER + SSD_CONV_DIM], w_dt, w_in[:, d0:]


def _mixers(projs, st, p, bsz, seq, pos0, precise):
    proj_a, proj_b, proj_c, dt_raw, proj_d = projs
    o_a, ret_s = _retention(proj_a, st['ret'], p['ret_norm_w'], bsz, seq, pos0, precise)
    o_b, lru_h = _rg_lru(proj_b, st['lru_conv'], st['lru_h'], p, bsz, seq, pos0, precise)
    o_c, ssd_s = _ssd(proj_c, dt_raw, st['ssd_conv'], st['ssd'], p, bsz, seq, precise)
    r, lw, k, v, nkk, bb, bonus, og = _rwkv_pre(proj_d, st['rwkv_shift'], p, bsz, seq, precise)
    o, rwkv_s = _rwkv_scan(r, lw, k, v, nkk, bb, st['rwkv'], bsz, seq, precise)
    o_d = _rwkv_post(o, bonus, og, p, precise)
    last = lambda a, nrow: a.reshape(bsz, seq, -1)[:, seq - nrow:]
    new = {'ret': ret_s, 'lru_h': lru_h, 'lru_conv': last(proj_b, 3)[:, :, :LRU_WIDTH], 'ssd': ssd_s,
           'ssd_conv': last(proj_c, 3)[:, :, SSD_INNER:], 'rwkv': rwkv_s, 'rwkv_shift': last(proj_d, 1)}
    return [o_a, o_b, o_c, o_d], new


def _layer(x, xb, groups, states, p, alpha, last):
    (bp, lp, pos_p), (bs, ls, pos_s) = groups
    n_p = bp * lp
    w_parts = _split_w_in(p['w_in'])
    x_s = x[n_p:]
    projs_p = [_matmul(xb, w.astype(BF16), n_p) for w in w_parts]
    projs_s = [_matmul_f32(x_s, w) for w in w_parts]
    mix_p, new_p = _mixers(projs_p, states[0], p, bp, lp, pos_p, False)
    mix_s, new_s = _mixers(projs_s, states[1], p, bs, ls, pos_s, True)
    x1, x1b = _out_ln(mix_p, mix_s, p['w_out'], x, p['ln1_g'], p['ln1_b'], alpha)
    out_a, out_b = _moe_ln(x1, x1b, p, alpha, n_p if last else None)
    return out_a, out_b, (new_p, new_s)


_PARAM_NAMES = ('w_in', 'w_out', 'ret_norm_w', 'lru_conv_w', 'lru_conv_b', 'lru_wa', 'lru_ba', 'lru_wx', 'lru_bx',
                'lru_lambda', 'lru_norm_w', 'ssd_conv_w', 'ssd_conv_b', 'ssd_dt_bias', 'ssd_A_log', 'ssd_D',
                'ssd_norm_w', 'rwkv_mu', 'rwkv_w0', 'rwkv_w2', 'rwkv_a0', 'rwkv_a2', 'rwkv_g2', 'rwkv_k_k',
                'rwkv_k_a', 'rwkv_r_k', 'rwkv_ln_w', 'rwkv_ln_b', 'ln1_g', 'ln1_b', 'router_w', 'router_bias',
                'exp_w_gate', 'exp_w_up', 'exp_w_down', 'sh_w_gate', 'sh_w_up', 'sh_w_down', 'ln2_g', 'ln2_b')
_STATE_NAMES = ('ret', 'lru_h', 'lru_conv', 'ssd', 'ssd_conv', 'rwkv', 'rwkv_shift')


def _forward(x_prompt, x_sample, states_s, params):
    depth = params['w_in'].shape[0]
    bp, lp, d = x_prompt.shape
    bs, ls, _ = x_sample.shape
    alpha = (2 * depth) ** 0.25
    groups = ((bp, lp, 0), (bs, ls, PAST_LEN))
    x = jnp.concatenate([x_prompt.reshape(bp * lp, d), x_sample.reshape(bs * ls, d)], axis=0)
    xb = x.astype(BF16)
    out_p = {n: [] for n in _STATE_NAMES}
    out_s = {n: [] for n in _STATE_NAMES}
    for l in range(depth):
        p = {n: params[n][l] for n in _PARAM_NAMES}
        st_s = {n: states_s[n][l] for n in _STATE_NAMES}
        st_p = {n: jnp.zeros((bp,) + st_s[n].shape[1:], F32) for n in _STATE_NAMES}
        out_a, out_b, (new_p, new_s) = _layer(x, xb, groups, (st_p, st_s), p, alpha, l == depth - 1)
        x, xb = out_a, out_b
        for n in _STATE_NAMES:
            out_p[n].append(new_p[n])
            out_s[n].append(new_s[n])
    stk = lambda lst: jnp.stack(lst, axis=0)
    outs = [out_a.reshape(bp, lp, d), out_b.reshape(bs, ls, d)]
    for n in _STATE_NAMES:
        outs += [stk(out_p[n]), stk(out_s[n])]
    return tuple(outs)


def kernel(x_prompt, x_sample, state_ret, state_lru, cache_lru_conv, state_ssm, cache_ssm_conv, state_rwkv, cache_rwkv_shift, w_in, w_out, ret_norm_w, lru_conv_w, lru_conv_b, lru_wa, lru_ba, lru_wx, lru_bx, lru_lambda, lru_norm_w, ssd_conv_w, ssd_conv_b, ssd_dt_bias, ssd_A_log, ssd_D, ssd_norm_w, rwkv_mu, rwkv_w0, rwkv_w2, rwkv_a0, rwkv_a2, rwkv_g2, rwkv_k_k, rwkv_k_a, rwkv_r_k, rwkv_ln_w, rwkv_ln_b, ln1_g, ln1_b, router_w, router_bias, exp_w_gate, exp_w_up, exp_w_down, sh_w_gate, sh_w_up, sh_w_down, ln2_g, ln2_b):
    states_s = {'ret': state_ret, 'lru_h': state_lru, 'lru_conv': cache_lru_conv, 'ssd': state_ssm,
                'ssd_conv': cache_ssm_conv, 'rwkv': state_rwkv, 'rwkv_shift': cache_rwkv_shift}
    params = dict(zip(_PARAM_NAMES, (
        w_in, w_out, ret_norm_w, lru_conv_w, lru_conv_b, lru_wa, lru_ba, lru_wx, lru_bx, lru_lambda, lru_norm_w,
        ssd_conv_w, ssd_conv_b, ssd_dt_bias, ssd_A_log, ssd_D, ssd_norm_w, rwkv_mu, rwkv_w0, rwkv_w2, rwkv_a0,
        rwkv_a2, rwkv_g2, rwkv_k_k, rwkv_k_a, rwkv_r_k, rwkv_ln_w, rwkv_ln_b, ln1_g, ln1_b, router_w, router_bias,
        exp_w_gate, exp_w_up, exp_w_down, sh_w_gate, sh_w_up, sh_w_down, ln2_g, ln2_b)))
    return _forward(x_prompt, x_sample, states_s, params)
```

```python
import functools
import math

import numpy as np
import jax
import jax.numpy as jnp
from jax import lax
from jax.experimental import pallas as pl
from jax.experimental.pallas import tpu as pltpu

F32 = jnp.float32
BF16 = jnp.bfloat16
I32 = jnp.int32

D_MODEL = 2048
PAST_LEN = 1024
RET_HEADS, RET_DK, RET_DV = 4, 64, 128
LRU_WIDTH, LRU_BLOCKS, LRU_C = 512, 4, 8.0
SSD_INNER, SSD_HEADDIM, SSD_HEADS, SSD_GROUPS, SSD_STATE = 512, 64, 8, 2, 128
RWKV_WIDTH, RWKV_HEADDIM, RWKV_HEADS = 512, 64, 8
RWKV_LN_EPS = 64e-5
N_EXPERTS, TOP_K, N_EXPERT_GROUPS, TOPK_GROUPS = 64, 8, 8, 4
D_EXPERT = 512
ROUTED_SCALE = 2.5
LN_EPS = 1e-5
RMS_EPS = 1e-6
A_QK = RET_HEADS * RET_DK
A_V = RET_HEADS * RET_DV
COLS_A = 2 * A_QK + 2 * A_V
COLS_B = 2 * LRU_WIDTH
SSD_CONV_DIM = SSD_INNER + 2 * SSD_GROUPS * SSD_STATE
COLS_C = SSD_INNER + SSD_CONV_DIM + SSD_HEADS
COLS_D = 3 * RWKV_WIDTH + 64 + 64 + 128

VMEM_LIMIT_BYTES = 56 * 1024 * 1024
LANES = 128
SUBLANES = 8

TOK_TILE = 512
LN_TILE = 256
MOE_ROWS = 512
COMBINE_TILE = 64
DISPATCH_TILE = 256
DEST_TILE = 4096
ROUTER_TILE = 256
RET_CHUNK = 256
LRU_CHUNK = 256
SSD_CHUNK = 128
RWKV_CHUNK = 64
RWKV_PRE_TILE = 256

HIGHEST = lax.Precision.HIGHEST


def _cp(sem, vmem=VMEM_LIMIT_BYTES):
    return pltpu.CompilerParams(dimension_semantics=sem, vmem_limit_bytes=vmem)


def _sds(shape, dtype):
    return jax.ShapeDtypeStruct(shape, dtype)


def _full(shape):
    nd = len(shape)
    return pl.BlockSpec(shape, lambda *_: (0,) * nd)


def _tile(n, pref, mult=16):
    for t in range(min(pref, n) // mult * mult, 0, -mult):
        if n % t == 0:
            return t
    return n


def _sigmoid(x):
    return jax.nn.sigmoid(x)


def _silu(x):
    return x * jax.nn.sigmoid(x)


def _softplus(x):
    return jnp.maximum(x, 0.0) + jnp.log1p(jnp.exp(-jnp.abs(x)))


class _Dots:
    def __init__(self, precise):
        self.precise = precise

    def _dg(self, a, b, ca, cb):
        if self.precise:
            return lax.dot_general(a, b, (((ca,), (cb,)), ((), ())), preferred_element_type=F32, precision=HIGHEST)
        return lax.dot_general(a.astype(BF16), b.astype(BF16), (((ca,), (cb,)), ((), ())),
                               preferred_element_type=F32)

    def nn(self, a, b):
        return self._dg(a, b, 1, 0)

    def nt(self, a, b):
        return self._dg(a, b, 1, 1)

    def tn(self, a, b):
        return self._dg(a, b, 0, 0)


def _fdot(a, b):
    return jnp.dot(a, b, preferred_element_type=F32, precision=HIGHEST)


def _mm_body(x_ref, w_ref, o_ref):
    o_ref[...] = jnp.dot(x_ref[...], w_ref[...], preferred_element_type=F32)


def _matmul(x, w, rows):
    k = x.shape[1]
    n = w.shape[1]
    tm = _tile(rows, TOK_TILE)
    return pl.pallas_call(
        _mm_body, grid=(rows // tm,),
        in_specs=[pl.BlockSpec((tm, k), lambda i: (i, 0)), _full((k, n))],
        out_specs=pl.BlockSpec((tm, n), lambda i: (i, 0)),
        out_shape=_sds((rows, n), F32), compiler_params=_cp(("parallel",)), name="proj_matmul")(x, w)


def _mm_f32_body(x_ref, w_ref, o_ref):
    o_ref[...] = _fdot(x_ref[...], w_ref[...])


def _matmul_f32(x, w):
    m, k = x.shape
    n = w.shape[1]
    tn = _tile(n, 2 * LANES, LANES)
    return pl.pallas_call(
        _mm_f32_body, grid=(n // tn,),
        in_specs=[_full((m, k)), pl.BlockSpec((k, tn), lambda j: (0, j))],
        out_specs=pl.BlockSpec((m, tn), lambda j: (0, j)),
        out_shape=_sds((m, n), F32), compiler_params=_cp(("parallel",)), name="proj_matmul_f32")(x, w)


def _ret_body(q_ref, k_ref, v_ref, g_ref, cos_ref, sin_ref, dmask_ref, qdec_ref, kdec_ref, cdec_ref,
              nw_ref, s0_ref, o_ref, sout_ref, s_scr, *, precise):
    c = pl.program_id(1)
    dot = _Dots(precise)

    @pl.when(c == 0)
    def _():
        s_scr[...] = s0_ref[0]

    t = q_ref.shape[0]
    lane = lax.broadcasted_iota(I32, (t, LANES), 1)
    first = (lane % RET_DK) < (RET_DK // 2)
    cos = cos_ref[...]
    sin = sin_ref[...]

    def rot(x):
        other = jnp.where(first, pltpu.roll(x, LANES - RET_DK // 2, 1), pltpu.roll(x, RET_DK // 2, 1))
        return x * cos + other * sin

    qs = [rot(q_ref[:, p * LANES:(p + 1) * LANES]) for p in range(2)]
    ks = [rot(k_ref[:, p * LANES:(p + 1) * LANES]) * (RET_DK ** -0.5) for p in range(2)]
    kds = [ks[p] * kdec_ref[p] for p in range(2)]
    for h in range(RET_HEADS):
        lo = (h % 2) * RET_DK
        qh = qs[h // 2][:, lo:lo + RET_DK]
        kh = ks[h // 2][:, lo:lo + RET_DK]
        kdh = kds[h // 2][:, lo:lo + RET_DK]
        vh = v_ref[:, h * RET_DV:(h + 1) * RET_DV]
        s = s_scr[h]
        att = dot.nt(qh, kh) * dmask_ref[h]
        o = dot.nn(att, vh) + dot.nn(qh, s) * qdec_ref[h]
        s_scr[h] = s * cdec_ref[h] + dot.tn(kdh, vh)
        on = o * lax.rsqrt(jnp.mean(o * o, axis=-1, keepdims=True) + RMS_EPS)
        sl = slice(h * RET_DV, (h + 1) * RET_DV)
        o_ref[:, sl] = (on * nw_ref[:, sl] * _silu(g_ref[:, sl])).astype(o_ref.dtype)

    @pl.when(c == pl.num_programs(1) - 1)
    def _():
        sout_ref[0] = s_scr[...]


def _ret_tables(seq, t, pos0):
    half = RET_DK // 2
    freq = 10000.0 ** (-jnp.linspace(0.0, 1.0, half, dtype=F32))
    pos = (pos0 + jnp.arange(seq, dtype=jnp.int32)).astype(F32)
    ang = pos[:, None] * freq
    cos = jnp.cos(ang)
    sin = jnp.sin(ang)
    cos_t = jnp.tile(cos, (1, LANES // half))
    sin_t = jnp.tile(jnp.concatenate([-sin, sin], axis=1), (1, LANES // RET_DK))
    lg = jnp.log1p(-(2.0 ** (-5.0 - jnp.arange(RET_HEADS, dtype=F32))))
    idx = jnp.arange(t, dtype=F32)
    rel = idx[:, None] - idx[None, :]
    causal = rel >= 0
    dmask = jnp.where(causal[None], jnp.exp(jnp.where(causal, rel, 0.0)[None] * lg[:, None, None]), 0.0)
    q_dec = jnp.exp((idx[None, :] + 1.0) * lg[:, None])
    k_dec = jnp.exp((t - 1.0 - idx[None, :]) * lg[:, None])
    c_dec = jnp.exp(t * lg)
    qdec = jnp.broadcast_to(q_dec[:, :, None], (RET_HEADS, t, RET_DV))
    kdec = jnp.repeat(k_dec.T, RET_DK, axis=1).reshape(t, 2, LANES).transpose(1, 0, 2)
    cdec = jnp.broadcast_to(c_dec[:, None, None], (RET_HEADS, 1, RET_DV))
    return cos_t, sin_t, dmask, qdec, kdec, cdec


def _retention(proj_a, s0, norm_w, bsz, seq, pos0, precise):
    t = min(RET_CHUNK, seq)
    nc = seq // t
    cos_t, sin_t, dmask, qdec, kdec, cdec = _ret_tables(seq, t, pos0)

    def rows(w, col):
        return pl.BlockSpec((t, w), lambda b, c: (b * nc + c, col))

    o, s_out = pl.pallas_call(
        functools.partial(_ret_body, precise=precise), grid=(bsz, nc),
        in_specs=[rows(A_QK, 0), rows(A_QK, 1), rows(A_V, 1), rows(A_V, 2),
                  pl.BlockSpec((t, LANES), lambda b, c: (c, 0)),
                  pl.BlockSpec((t, LANES), lambda b, c: (c, 0)),
                  _full((RET_HEADS, t, t)), _full((RET_HEADS, t, RET_DV)), _full((2, t, LANES)),
                  _full((RET_HEADS, 1, RET_DV)), _full((1, A_V)),
                  pl.BlockSpec((1, RET_HEADS, RET_DK, RET_DV), lambda b, c: (b, 0, 0, 0))],
        out_specs=[pl.BlockSpec((t, A_V), lambda b, c: (b * nc + c, 0)),
                   pl.BlockSpec((1, RET_HEADS, RET_DK, RET_DV), lambda b, c: (b, 0, 0, 0))],
        out_shape=[_sds((bsz * seq, A_V), F32 if precise else BF16), _sds((bsz, RET_HEADS, RET_DK, RET_DV), F32)],
        scratch_shapes=[pltpu.VMEM((RET_HEADS, RET_DK, RET_DV), F32)],
        compiler_params=_cp(("parallel", "arbitrary")), name="retention",
    )(proj_a, proj_a, proj_a, proj_a, cos_t, sin_t, dmask, qdec, kdec, cdec, norm_w.reshape(1, A_V), s0)
    return o, s_out


def _conv_chunk(x, xp_scr, cw_ref, cb_ref, t):
    kw = cw_ref.shape[0]
    xp_scr[SUBLANES:SUBLANES + t, :] = x
    y = cb_ref[...] + xp_scr[SUBLANES - kw + 1:SUBLANES - kw + 1 + t, :] * cw_ref[0:1, :]
    for j in range(1, kw):
        off = SUBLANES - kw + 1 + j
        y = y + xp_scr[off:off + t, :] * cw_ref[j:j + 1, :]
    xp_scr[0:SUBLANES, :] = xp_scr[t:t + SUBLANES, :]
    return y


def _pad_conv_state(buf):
    return jnp.pad(buf, ((0, 0), (SUBLANES - buf.shape[1], 0), (0, 0)))


def _lru_body(xb_ref, gb_ref, cbuf_ref, h0_ref, cw_ref, cb_ref, wa_ref, ba_ref, wx_ref, bx_ref, lam_ref,
              nw_ref, o_ref, hout_ref, xp_scr, h_scr, *, first_pos_is_zero, precise):
    c = pl.program_id(1)
    dot = _Dots(precise)
    t = xb_ref.shape[0]

    @pl.when(c == 0)
    def _():
        xp_scr[0:SUBLANES, :] = cbuf_ref[0]
        h_scr[...] = h0_ref[0]

    xc = _conv_chunk(xb_ref[...], xp_scr, cw_ref, cb_ref, t)
    bw = LRU_WIDTH // LRU_BLOCKS
    ra, ri = [], []
    for n in range(LRU_BLOCKS):
        xh = xc[:, n * bw:(n + 1) * bw]
        ra.append(dot.nn(xh, wa_ref[n]))
        ri.append(dot.nn(xh, wx_ref[n]))
    r = _sigmoid(jnp.concatenate(ra, axis=1) + ba_ref[...])
    i = _sigmoid(jnp.concatenate(ri, axis=1) + bx_ref[...])
    log_a = -LRU_C * r * _softplus(-lam_ref[...])
    a = jnp.exp(log_a)
    th = jnp.tanh(log_a)
    mult = jnp.sqrt(-2.0 * th / (1.0 - th))
    row = lax.broadcasted_iota(I32, (t, LRU_WIDTH), 0)
    if first_pos_is_zero:
        mult = jnp.where((row + c * t) == 0, 1.0, mult)
    b = mult * (i * xc)
    s = 1
    while s < t:
        keep = row >= s
        a_sh = jnp.where(keep, pltpu.roll(a, s, 0), 1.0)
        b_sh = jnp.where(keep, pltpu.roll(b, s, 0), 0.0)
        b = a * b_sh + b
        a = a * a_sh
        s *= 2
    h = a * h_scr[...] + b
    h_scr[...] = h[t - 1:t, :]
    gb = gb_ref[...]
    gelu = 0.5 * gb * (1.0 + jnp.tanh(math.sqrt(2.0 / math.pi) * (gb + 0.044715 * (gb * gb * gb))))
    y = h * gelu
    yn = y * lax.rsqrt(jnp.mean(y * y, axis=-1, keepdims=True) + RMS_EPS)
    o_ref[...] = (yn * nw_ref[...]).astype(o_ref.dtype)

    @pl.when(c == pl.num_programs(1) - 1)
    def _():
        hout_ref[0] = h_scr[...]


def _rg_lru(proj_b, cbuf, h0, p, bsz, seq, pos0, precise):
    t = min(LRU_CHUNK, seq)
    nc = seq // t
    w = LRU_WIDTH
    bw = w // LRU_BLOCKS
    row1 = lambda v: v.reshape(1, w)
    o, h_out = pl.pallas_call(
        functools.partial(_lru_body, first_pos_is_zero=(pos0 == 0), precise=precise), grid=(bsz, nc),
        in_specs=[pl.BlockSpec((t, w), lambda b, c: (b * nc + c, 0)),
                  pl.BlockSpec((t, w), lambda b, c: (b * nc + c, 1)),
                  pl.BlockSpec((1, SUBLANES, w), lambda b, c: (b, 0, 0)),
                  pl.BlockSpec((1, 1, w), lambda b, c: (b, 0, 0)),
                  _full((4, w)), _full((1, w)), _full((LRU_BLOCKS, bw, bw)), _full((1, w)),
                  _full((LRU_BLOCKS, bw, bw)), _full((1, w)), _full((1, w)), _full((1, w))],
        out_specs=[pl.BlockSpec((t, w), lambda b, c: (b * nc + c, 0)),
                   pl.BlockSpec((1, 1, w), lambda b, c: (b, 0, 0))],
        out_shape=[_sds((bsz * seq, w), F32 if precise else BF16), _sds((bsz, 1, w), F32)],
        scratch_shapes=[pltpu.VMEM((t + SUBLANES, w), F32), pltpu.VMEM((1, w), F32)],
        compiler_params=_cp(("parallel", "arbitrary")), name="rg_lru",
    )(proj_b, proj_b, _pad_conv_state(cbuf), h0.reshape(bsz, 1, w), p['lru_conv_w'], row1(p['lru_conv_b']),
      p['lru_wa'], row1(p['lru_ba']), p['lru_wx'], row1(p['lru_bx']),
      row1(p['lru_lambda']), row1(p['lru_norm_w']))
    return o, h_out.reshape(bsz, w)


def _ssd_body(z_ref, xs_ref, bc_ref, dt_ref, dtt_ref, cbuf_ref, s0_ref, cw_ref, cb_ref, dtb_ref, dtbc_ref,
              arow_ref, acol_ref, dexp_ref, nw_ref, tril_ref, triu_ref, o_ref, sout_ref, xp_scr, st_scr, *, precise):
    c = pl.program_id(1)
    dot = _Dots(precise)
    t = z_ref.shape[0]
    npair = SSD_HEADS // 2

    @pl.when(c == 0)
    def _():
        xp_scr[0:SUBLANES, :] = cbuf_ref[0]
        for p in range(npair):
            st_scr[p] = s0_ref[0, 2 * p:2 * p + 2].reshape(LANES, SSD_STATE).T

    xin = jnp.concatenate([xs_ref[...], bc_ref[...]], axis=1)
    xbc = _silu(_conv_chunk(xin, xp_scr, cw_ref, cb_ref, t))
    xs = xbc[:, :SSD_INNER]
    gs = SSD_GROUPS * SSD_STATE
    bm = [xbc[:, SSD_INNER + g * SSD_STATE:SSD_INNER + (g + 1) * SSD_STATE] for g in range(SSD_GROUPS)]
    cm = [xbc[:, SSD_INNER + gs + g * SSD_STATE:SSD_INNER + gs + (g + 1) * SSD_STATE] for g in range(SSD_GROUPS)]
    dt_c = _softplus(dt_ref[...] + dtb_ref[...])
    dt_r = _softplus(dtt_ref[0] + dtbc_ref[:, :t])
    cum_c = _fdot(tril_ref[...], dt_c * arow_ref[...])
    cum_r = _fdot(dt_r * acol_ref[:, :t], triu_ref[...])
    cb = [dot.nt(cm[g], bm[g]) for g in range(SSD_GROUPS)]
    rowi = lax.broadcasted_iota(I32, (t, t), 0)
    coli = lax.broadcasted_iota(I32, (t, t), 1)
    tri = rowi >= coli
    lane = lax.broadcasted_iota(I32, (t, LANES), 1)
    lo = lane < SSD_HEADDIM
    ys = []
    for p in range(npair):
        g = (2 * p) // (SSD_HEADS // SSD_GROUPS)
        x_pair = xs[:, p * LANES:(p + 1) * LANES]
        y_pair = jnp.zeros((t, LANES), F32)
        cc, cend, te = [], [], []
        for hh in range(2):
            h = 2 * p + hh
            cch = cum_c[:, h:h + 1]
            seg = cch - cum_r[h:h + 1, :]
            lmat = jnp.where(tri, jnp.exp(jnp.where(tri, seg, 0.0)), 0.0)
            m = cb[g] * lmat * dt_r[h:h + 1, :]
            xm = jnp.where(lo if hh == 0 else jnp.logical_not(lo), x_pair, 0.0)
            y_pair = y_pair + dot.nn(m, xm)
            ce = cum_c[t - 1:t, h:h + 1]
            cc.append(cch)
            cend.append(ce)
            te.append(jnp.exp(ce - cch) * dt_c[:, h:h + 1])
        st = st_scr[p]
        y_pair = y_pair + dot.nn(cm[g], st) * jnp.where(lo, jnp.exp(cc[0]), jnp.exp(cc[1]))
        xt = x_pair * jnp.where(lo, te[0], te[1])
        st_scr[p] = st * jnp.where(lo[0:1, :], jnp.exp(cend[0]), jnp.exp(cend[1])) + dot.tn(bm[g], xt)
        ys.append(y_pair)
    y = jnp.concatenate(ys, axis=1)
    z = z_ref[...]
    y = (y + xs * dexp_ref[...]) * _silu(z)
    gw = SSD_INNER // SSD_GROUPS
    outs = []
    for g in range(SSD_GROUPS):
        yg = y[:, g * gw:(g + 1) * gw]
        outs.append(yg * lax.rsqrt(jnp.mean(yg * yg, axis=-1, keepdims=True) + RMS_EPS))
    o_ref[...] = (jnp.concatenate(outs, axis=1) * nw_ref[...]).astype(o_ref.dtype)

    @pl.when(c == pl.num_programs(1) - 1)
    def _():
        for p in range(npair):
            sout_ref[0, 2 * p:2 * p + 2] = st_scr[p].T.reshape(2, SSD_HEADDIM, SSD_STATE)


def _ssd(proj_c, dt_raw, cbuf, s0, p, bsz, seq, precise):
    t = min(SSD_CHUNK, seq)
    nc = seq // t
    w = SSD_INNER
    cw = SSD_CONV_DIM
    dtt = jnp.swapaxes(dt_raw[:, :SSD_HEADS].reshape(bsz, seq, SSD_HEADS), 1, 2)
    dtb = p['ssd_dt_bias'].astype(F32)
    a = -jnp.exp(p['ssd_A_log'].astype(F32))
    pad8 = lambda v: jnp.pad(v, (0, LANES - SSD_HEADS)).reshape(1, LANES)
    col8 = lambda v: jnp.broadcast_to(v[:, None], (SSD_HEADS, LANES))
    dexp = jnp.repeat(p['ssd_D'].astype(F32), SSD_HEADDIM).reshape(1, w)
    tril = jnp.tril(jnp.ones((t, t), F32))
    o, s_out = pl.pallas_call(
        functools.partial(_ssd_body, precise=precise), grid=(bsz, nc),
        in_specs=[pl.BlockSpec((t, w), lambda b, c: (b * nc + c, 0)),
                  pl.BlockSpec((t, w), lambda b, c: (b * nc + c, 1)),
                  pl.BlockSpec((t, w), lambda b, c: (b * nc + c, 2)),
                  pl.BlockSpec((t, LANES), lambda b, c: (b * nc + c, 0)),
                  pl.BlockSpec((1, SSD_HEADS, t), lambda b, c: (b, 0, c)),
                  pl.BlockSpec((1, SUBLANES, cw), lambda b, c: (b, 0, 0)),
                  pl.BlockSpec((1, SSD_HEADS, SSD_HEADDIM, SSD_STATE), lambda b, c: (b, 0, 0, 0)),
                  _full((4, cw)), _full((1, cw)), _full((1, LANES)), _full((SSD_HEADS, LANES)),
                  _full((1, LANES)), _full((SSD_HEADS, LANES)), _full((1, w)), _full((1, w)),
                  _full((t, t)), _full((t, t))],
        out_specs=[pl.BlockSpec((t, w), lambda b, c: (b * nc + c, 0)),
                   pl.BlockSpec((1, SSD_HEADS, SSD_HEADDIM, SSD_STATE), lambda b, c: (b, 0, 0, 0))],
        out_shape=[_sds((bsz * seq, w), F32 if precise else BF16), _sds((bsz, SSD_HEADS, SSD_HEADDIM, SSD_STATE), F32)],
        scratch_shapes=[pltpu.VMEM((t + SUBLANES, cw), F32), pltpu.VMEM((SSD_HEADS // 2, SSD_STATE, LANES), F32)],
        compiler_params=_cp(("parallel", "arbitrary")), name="ssd",
    )(proj_c, proj_c, proj_c, dt_raw, dtt, _pad_conv_state(cbuf), s0, p['ssd_conv_w'],
      p['ssd_conv_b'].reshape(1, cw), pad8(dtb), col8(dtb), pad8(a), col8(a), dexp,
      p['ssd_norm_w'].reshape(1, w), tril, tril.T)
    return o, s_out


def _rwkv_pre_body(pd_ref, prev_ref, shift_ref, mu_ref, wlora_ref, g2_ref, w0_ref, a0_ref, kk_ref, ka_ref,
                   rk_ref, ones_ref, r_ref, lw_ref, k_ref, v_ref, nkk_ref, bb_ref, bonus_ref, og_ref, *, precise):
    c = pl.program_id(1)
    dot = _Dots(precise)
    t = pd_ref.shape[0]
    w = RWKV_WIDTH
    pd = pd_ref[...]
    prev_row = jnp.where(c == 0, shift_ref[0], prev_ref[SUBLANES - 1:SUBLANES, :])
    row = lax.broadcasted_iota(I32, pd.shape, 0)
    prev = jnp.where(row == 0, prev_row, pltpu.roll(pd, 1, 0))
    mixed = pd + (prev - pd) * mu_ref[...]
    r = mixed[:, 0:w]
    kd = mixed[:, w:2 * w]
    vd = mixed[:, 2 * w:3 * w]
    lora_in = mixed[:, 3 * w:3 * w + LANES]
    lane = lax.broadcasted_iota(I32, lora_in.shape, 1)
    lora_in = jnp.where(lane < 64, jnp.tanh(lora_in), lora_in)
    lora = dot.nn(lora_in, wlora_ref[...])
    gd = mixed[:, 3 * w + LANES:3 * w + 2 * LANES]
    og_ref[...] = dot.nn(_sigmoid(gd), g2_ref[...])
    w_log = -_softplus(-(w0_ref[...] + lora[:, :w])) - 0.5
    lw_ref[...] = -jnp.exp(w_log)
    iclr = _sigmoid(a0_ref[...] + lora[:, w:])
    kk = kd * kk_ref[...]
    ss = _fdot(kk * kk, ones_ref[...])
    kk = kk / jnp.maximum(jnp.sqrt(ss), 1e-12)
    kr = kd * (1.0 + (iclr - 1.0) * ka_ref[...])
    r_ref[...] = r
    k_ref[...] = kr
    v_ref[...] = vd
    nkk_ref[...] = -kk
    bb_ref[...] = kk * iclr
    bonus_ref[...] = _fdot(r * kr * rk_ref[...], ones_ref[...]) * vd


def _head_ones():
    hid = jnp.arange(RWKV_WIDTH) // RWKV_HEADDIM
    return (hid[:, None] == hid[None, :]).astype(F32)


def _rwkv_pre(proj_d, shift, p, bsz, seq, precise):
    t = min(RWKV_PRE_TILE, seq)
    nc = seq // t
    w = RWKV_WIDTH
    tb = t // SUBLANES
    wlora = jnp.zeros((LANES, 2 * w), F32).at[:64, :w].set(p['rwkv_w2']).at[64:, w:].set(p['rwkv_a2'])
    row1 = lambda v: v.reshape(1, -1)
    outs = pl.pallas_call(
        functools.partial(_rwkv_pre_body, precise=precise), grid=(bsz, nc),
        in_specs=[pl.BlockSpec((t, COLS_D), lambda b, c: (b * nc + c, 0)),
                  pl.BlockSpec((SUBLANES, COLS_D), lambda b, c: (jnp.maximum((b * nc + c) * tb - 1, 0), 0)),
                  pl.BlockSpec((1, 1, COLS_D), lambda b, c: (b, 0, 0)),
                  _full((1, COLS_D)), _full((LANES, 2 * w)), _full((LANES, w)), _full((1, w)), _full((1, w)),
                  _full((1, w)), _full((1, w)), _full((1, w)), _full((w, w))],
        out_specs=[pl.BlockSpec((t, w), lambda b, c: (b * nc + c, 0))] * 8,
        out_shape=[_sds((bsz * seq, w), F32)] * 8,
        compiler_params=_cp(("parallel", "parallel")), name="rwkv_pre",
    )(proj_d, proj_d, shift, row1(p['rwkv_mu']), wlora, p['rwkv_g2'], row1(p['rwkv_w0']),
      row1(p['rwkv_a0']), row1(p['rwkv_k_k']), row1(p['rwkv_k_a']), row1(p['rwkv_r_k']), _head_ones())
    return outs


def _rwkv_scan_body(r_ref, lw_ref, k_ref, v_ref, a_ref, b_ref, s0_ref, tril_ref, o_ref, sout_ref, s_scr, *, precise):
    c = pl.program_id(1)
    dot = _Dots(precise)
    hd = RWKV_HEADDIM
    gw = 4 * hd

    @pl.when(c == 0)
    def _():
        s_scr[...] = s0_ref[...]

    row = lax.broadcasted_iota(I32, (hd, gw), 0)
    lane = lax.broadcasted_iota(I32, (hd, gw), 1)
    col = jnp.bitwise_and(lane, hd - 1)
    strict = col < row
    incl = col <= row
    eye = jnp.where(col == row, 1.0, 0.0)
    lane_blk = jnp.right_shift(lane, 6)
    bd_mask = (jnp.right_shift(lax.broadcasted_iota(I32, (gw, gw), 0), 6)
               == jnp.right_shift(lax.broadcasted_iota(I32, (gw, gw), 1), 6))

    def bd(x):
        xo = x if precise else x.astype(BF16)
        return jnp.where(bd_mask, jnp.concatenate([xo] * 4, axis=0), jnp.zeros((), xo.dtype))

    tril = tril_ref[...]
    for bb in range(2):
        for g in range(RWKV_HEADS // 4):
            sl = slice(g * gw, (g + 1) * gw)
            r, lw, k, v = r_ref[bb, :, sl], lw_ref[bb, :, sl], k_ref[bb, :, sl], v_ref[bb, :, sl]
            a, b = a_ref[bb, :, sl], b_ref[bb, :, sl]
            cum = _fdot(tril, lw)
            at = a * jnp.exp(cum - lw)
            rt = r * jnp.exp(cum)
            inv = jnp.exp(-cum)
            bt = b * inv
            kt = k * inv
            to_end = jnp.exp(cum[hd - 1:hd, :] - cum)
            gram = dot.nt(jnp.concatenate([at, rt], axis=0), jnp.concatenate([bd(bt), bd(kt)], axis=0))
            l_ab = jnp.where(strict, gram[:hd, :gw], 0.0)
            l_ak = jnp.where(strict, gram[:hd, gw:], 0.0)
            m_rb = jnp.where(incl, gram[hd:, :gw], 0.0)
            m_rk = jnp.where(incl, gram[hd:, gw:], 0.0)
            t_inv = eye + l_ab
            lp = dot.nn(l_ab, bd(l_ab))
            for level in range(5):
                both = dot.nn(jnp.concatenate([t_inv, lp], axis=0), bd(lp))
                t_inv = t_inv + both[:hd]
                lp = both[hd:]
            bd_v = bd(v)
            z = dot.nn(l_ak, bd_v)
            gu = dot.nn(t_inv, jnp.concatenate([bd(at), bd(z)], axis=1))
            g_mat, u_loc = gu[:, :gw], gu[:, gw:]
            qo = dot.nn(m_rb, jnp.concatenate([bd(g_mat), bd(u_loc)], axis=1))
            q_hat = rt + qo[:, :gw]
            o_loc = qo[:, gw:] + dot.nn(m_rk, bd_v)
            s = s_scr[bb, :, sl]
            su = dot.nt(jnp.concatenate([q_hat, g_mat], axis=0), bd(s))
            o_ref[bb, :, sl] = su[:hd] + o_loc
            u = su[hd:] + u_loc
            full = dot.tn(jnp.concatenate([u, v], axis=0), jnp.concatenate([b * to_end, k * to_end], axis=0))
            s_new = s * jnp.exp(cum[hd - 1:hd, :])
            for h in range(4):
                s_new = s_new + jnp.where(lane_blk == h, full[h * hd:(h + 1) * hd, :], 0.0)
            s_scr[bb, :, sl] = s_new

    @pl.when(c == pl.num_programs(1) - 1)
    def _():
        sout_ref[...] = s_scr[...]


def _rwkv_scan(r, lw, k, v, nkk, bb, s0, bsz, seq, precise):
    wd = RWKV_WIDTH
    hd = RWKV_HEADDIM
    t = RWKV_CHUNK
    pad = (-seq) % t
    nc = (seq + pad) // t
    r3 = lambda x: jnp.pad(x.reshape(bsz, seq, wd), ((0, 0), (0, pad), (0, 0)))
    s0s = s0.transpose(0, 2, 1, 3).reshape(bsz, hd, wd)
    spec = pl.BlockSpec((2, t, wd), lambda b, c: (b, c, 0))
    sspec = pl.BlockSpec((2, hd, wd), lambda b, c: (b, 0, 0))
    o, s_out = pl.pallas_call(
        functools.partial(_rwkv_scan_body, precise=precise), grid=(bsz // 2, nc),
        in_specs=[spec] * 6 + [sspec, _full((t, t))], out_specs=[spec, sspec],
        out_shape=[_sds((bsz, seq + pad, wd), F32), _sds((bsz, hd, wd), F32)],
        scratch_shapes=[pltpu.VMEM((2, hd, wd), F32)],
        compiler_params=_cp(("parallel", "arbitrary")), name="rwkv_scan",
    )(r3(r), r3(lw), r3(k), r3(v), r3(nkk), r3(bb), s0s, jnp.tril(jnp.ones((t, t), F32)))
    s_fin = s_out.reshape(bsz, hd, RWKV_HEADS, hd).transpose(0, 2, 1, 3)
    return o[:, :seq].reshape(bsz * seq, wd), s_fin


def _rwkv_post_body(o_ref, bonus_ref, og_ref, lw_ref, lb_ref, ones_ref, out_ref):
    o = o_ref[...]
    inv = 1.0 / RWKV_HEADDIM
    mean = _fdot(o, ones_ref[...]) * inv
    xc = o - mean
    var = _fdot(xc * xc, ones_ref[...]) * inv
    y = xc * lax.rsqrt(var + RWKV_LN_EPS) * lw_ref[...] + lb_ref[...]
    out_ref[...] = ((y + bonus_ref[...]) * og_ref[...]).astype(out_ref.dtype)


def _rwkv_post(o, bonus, og, p, precise):
    n = o.shape[0]
    t = _tile(n, RWKV_PRE_TILE)
    w = RWKV_WIDTH
    spec = pl.BlockSpec((t, w), lambda i: (i, 0))
    return pl.pallas_call(
        _rwkv_post_body, grid=(n // t,),
        in_specs=[spec, spec, spec, _full((1, w)), _full((1, w)), _full((w, w))],
        out_specs=spec, out_shape=_sds((n, w), F32 if precise else BF16),
        compiler_params=_cp(("parallel",)), name="rwkv_post",
    )(o, bonus, og, p['rwkv_ln_w'].reshape(1, w), p['rwkv_ln_b'].reshape(1, w), _head_ones())


def _layer_norm_rows(z, g, b):
    zc = z - jnp.mean(z, axis=-1, keepdims=True)
    return zc * lax.rsqrt(jnp.mean(zc * zc, axis=-1, keepdims=True) + LN_EPS) * g + b


def _out_ln_body(*refs, alpha, prompt_tiles, parts):
    mixp_refs, mixs_refs = refs[:parts], refs[parts:2 * parts]
    w_hbm, x_ref, g_ref, b_ref, y_ref, yb_ref, wf_scr, wb_scr, sem = refs[2 * parts:]
    i = pl.program_id(0)
    width = mixp_refs[0].shape[1]

    @pl.when(i == 0)
    def _():
        copy = pltpu.make_async_copy(w_hbm, wf_scr, sem.at[0])
        copy.start()
        copy.wait()
        wb_scr[...] = wf_scr[...].astype(BF16)

    def finish(mix):
        y = _layer_norm_rows(alpha * x_ref[...] + mix, g_ref[...], b_ref[...])
        y_ref[...] = y
        yb_ref[...] = y.astype(BF16)

    def project(mix_refs, w_scr, dot):
        acc = dot(mix_refs[0][...], w_scr[0:width, :])
        for j in range(1, parts):
            acc = acc + dot(mix_refs[j][...], w_scr[j * width:(j + 1) * width, :])
        return acc

    @pl.when(i < prompt_tiles)
    def _():
        finish(project(mixp_refs, wb_scr, lambda a, b: jnp.dot(a, b, preferred_element_type=F32)))

    @pl.when(i >= prompt_tiles)
    def _():
        finish(project(mixs_refs, wf_scr, _fdot))


def _out_ln(mix_p, mix_s, w_out, x, g, b, alpha):
    n, d = x.shape
    parts = len(mix_p)
    n_p, n_s = mix_p[0].shape[0], mix_s[0].shape[0]
    width = mix_p[0].shape[1]
    tm = _tile(math.gcd(n_p, n_s), LN_TILE)
    pt = n_p // tm
    rows = lambda: pl.BlockSpec((tm, d), lambda i: (i, 0))
    spec_p = pl.BlockSpec((tm, width), lambda i: (jnp.minimum(i, pt - 1), 0))
    spec_s = pl.BlockSpec((tm, width), lambda i: (jnp.maximum(i - pt, 0), 0))
    return pl.pallas_call(
        functools.partial(_out_ln_body, alpha=alpha, prompt_tiles=pt, parts=parts), grid=(n // tm,),
        in_specs=[spec_p] * parts + [spec_s] * parts + [pl.BlockSpec(memory_space=pl.ANY), rows(), _full((1, d)),
                                                      _full((1, d))],
        out_specs=[rows(), rows()], out_shape=[_sds((n, d), F32), _sds((n, d), BF16)],
        scratch_shapes=[pltpu.VMEM((d, d), F32), pltpu.VMEM((d, d), BF16), pltpu.SemaphoreType.DMA((1,))],
        compiler_params=_cp(("arbitrary",)), name="out_proj_ln",
    )(*mix_p, *mix_s, w_out, x, g.reshape(1, d), b.reshape(1, d))


def _router_body(x_ref, wt_ref, bias_ref, tri_ref, idx_ref, gate_ref, rank_ref, cnt_ref, cnt_scr):
    i = pl.program_id(0)
    tm = x_ref.shape[0]
    ne, ng = N_EXPERTS, N_EXPERT_GROUPS
    per = ne // ng

    @pl.when(i == 0)
    def _():
        cnt_scr[...] = jnp.zeros(cnt_scr.shape, F32)

    logits = lax.dot_general(wt_ref[...], x_ref[...], (((1,), (1,)), ((), ())), preferred_element_type=F32,
                             precision=HIGHEST)
    scores = _sigmoid(logits)
    biased = scores + bias_ref[...]
    neg = -jnp.inf
    b3 = biased.reshape(ng, per, tm)
    e3 = lax.broadcasted_iota(I32, (ng, per, tm), 1)
    m1 = jnp.max(b3, axis=1, keepdims=True)
    i1 = jnp.min(jnp.where(b3 == m1, e3, per), axis=1, keepdims=True)
    m2 = jnp.max(jnp.where(e3 == i1, neg, b3), axis=1, keepdims=True)
    grp = (m1 + m2).reshape(ng, tm)
    gi = lax.broadcasted_iota(I32, (ng, tm), 0)
    keep = jnp.zeros((ng, tm), jnp.bool_)
    for _ in range(TOPK_GROUPS):
        m = jnp.max(grp, axis=0, keepdims=True)
        first = jnp.min(jnp.where(grp == m, gi, ng), axis=0, keepdims=True)
        sel = gi == first
        keep = jnp.logical_or(keep, sel)
        grp = jnp.where(sel, neg, grp)
    keep_e = jnp.broadcast_to(keep.astype(F32).reshape(ng, 1, tm), (ng, per, tm)).reshape(ne, tm) > 0.5
    masked = jnp.where(keep_e, biased, neg)
    ei = lax.broadcasted_iota(I32, (ne, tm), 0)
    sels, idxs, gates = [], [], []
    chosen = jnp.zeros((ne, tm), F32)
    for _ in range(TOP_K):
        m = jnp.max(masked, axis=0, keepdims=True)
        first = jnp.min(jnp.where(masked == m, ei, ne), axis=0, keepdims=True)
        sel = ei == first
        idxs.append(first)
        gates.append(jnp.sum(jnp.where(sel, scores, 0.0), axis=0, keepdims=True))
        sels.append(sel)
        chosen = chosen + sel.astype(F32)
        masked = jnp.where(sel, neg, masked)
    gsum = gates[0]
    for k in range(1, TOP_K):
        gsum = gsum + gates[k]
    before = jnp.dot(chosen.astype(BF16), tri_ref[...], preferred_element_type=F32) + cnt_scr[:, 0:tm]
    for k in range(TOP_K):
        idx_ref[k:k + 1, :] = idxs[k]
        gate_ref[k:k + 1, :] = gates[k] / gsum * ROUTED_SCALE
        rank_ref[k:k + 1, :] = jnp.sum(jnp.where(sels[k], before, 0.0), axis=0, keepdims=True).astype(I32)
    cnt_scr[...] = cnt_scr[...] + jnp.sum(chosen, axis=1, keepdims=True)
    cnt_ref[...] = cnt_scr[...]


def _router(x, router_w, router_bias):
    n, d = x.shape
    tm = _tile(n, ROUTER_TILE, LANES)
    tri = (jnp.arange(tm)[:, None] < jnp.arange(tm)[None, :]).astype(BF16)
    bias = jnp.broadcast_to(router_bias.astype(F32)[:, None], (N_EXPERTS, tm))
    tok = lambda: pl.BlockSpec((TOP_K, tm), lambda i: (0, i))
    return pl.pallas_call(
        _router_body, grid=(n // tm,),
        in_specs=[pl.BlockSpec((tm, d), lambda i: (i, 0)), _full((N_EXPERTS, d)), _full((N_EXPERTS, tm)),
                  _full((tm, tm))],
        out_specs=[tok(), tok(), tok(), _full((N_EXPERTS, tm))],
        out_shape=[_sds((TOP_K, n), I32), _sds((TOP_K, n), F32), _sds((TOP_K, n), I32), _sds((N_EXPERTS, tm), F32)],
        scratch_shapes=[pltpu.VMEM((N_EXPERTS, tm), F32)],
        compiler_params=_cp(("arbitrary",)), name="router",
    )(x, router_w.T, bias, tri)


def _ffn(xb, wg, wu, wd):
    h = _silu(jnp.dot(xb, wg, preferred_element_type=F32)) * jnp.dot(xb, wu, preferred_element_type=F32)
    return jnp.dot(h.astype(BF16), wd, preferred_element_type=F32)


def _shared_body(x_ref, wg_ref, wu_ref, wd_ref, o_ref):
    o_ref[...] = _ffn(x_ref[...], wg_ref[...], wu_ref[...], wd_ref[...])


def _shared_ffn(xb, wg, wu, wd):
    n, d = xb.shape
    tm = _tile(n, TOK_TILE)
    f = wg.shape[1]
    return pl.pallas_call(
        _shared_body, grid=(n // tm,),
        in_specs=[pl.BlockSpec((tm, d), lambda i: (i, 0)), _full((d, f)), _full((d, f)), _full((f, d))],
        out_specs=pl.BlockSpec((tm, d), lambda i: (i, 0)), out_shape=_sds((n, d), F32),
        compiler_params=_cp(("parallel",)), name="shared_ffn")(xb, wg, wu, wd)


def _dest_body(ps_ref, idx_ref, rank_ref, dest_ref):
    idx = idx_ref[...]
    base = lax.fori_loop(0, N_EXPERTS, lambda e, acc: jnp.where(idx == e, ps_ref[e], acc), jnp.zeros(idx.shape, I32))
    dest_ref[...] = base + rank_ref[...]


def _dest_rows(idx, rank, pad_start):
    n = idx.shape[1]
    tn = _tile(n, DEST_TILE, LANES)
    spec = lambda: pl.BlockSpec((TOP_K, tn), lambda i, ps: (0, i))
    grid_spec = pltpu.PrefetchScalarGridSpec(num_scalar_prefetch=1, grid=(n // tn,), in_specs=[spec(), spec()],
                                             out_specs=spec())
    return pl.pallas_call(_dest_body, grid_spec=grid_spec, out_shape=_sds((TOP_K, n), I32),
                          compiler_params=_cp(("parallel",)), name="moe_dest")(pad_start, idx, rank)


def _dispatch_body(ps_ref, pe_ref, nu_ref, dest_ref, x_ref, zero_hbm, xs_hbm, sem, zsem, *, tile, rows, nb):
    i = pl.program_id(0)

    def zero_block(first_row):
        dst = xs_hbm.at[pl.ds(pl.multiple_of(first_row, rows), rows)]
        return pltpu.make_async_copy(zero_hbm, dst, zsem.at[0])

    def for_zero_blocks(act):
        def per_expert(e, carry):
            @pl.when(pe_ref[e] > ps_ref[e])
            def _():
                act(zero_block(pe_ref[e] - rows))
            return carry

        def per_block(blk, carry):
            @pl.when(blk >= nu_ref[0])
            def _():
                act(zero_block(blk * rows))
            return carry

        lax.fori_loop(0, N_EXPERTS, per_expert, 0)
        lax.fori_loop(0, nb, per_block, 0)

    @pl.when(i == 0)
    def _():
        for_zero_blocks(lambda cp: cp.start())
        for_zero_blocks(lambda cp: cp.wait())

    for k in range(TOP_K):
        def one(t, carry):
            row = dest_ref[0, 0, k * tile + t]
            pltpu.make_async_copy(x_ref.at[pl.ds(t, 1)], xs_hbm.at[pl.ds(row, 1)], sem.at[0]).start()
            return carry
        lax.fori_loop(0, tile, one, 0, unroll=8)

    for k in range(TOP_K):
        pltpu.make_async_copy(x_ref, xs_hbm.at[pl.ds(0, tile)], sem.at[0]).wait()


def _tile_major(a, tile):
    n = a.shape[1]
    return a.reshape(TOP_K, n // tile, tile).transpose(1, 0, 2).reshape(n // tile, 1, TOP_K * tile)


def _dispatch(x, dest, pad_start, pad_end, n_used, nb):
    n, d = x.shape
    rows = MOE_ROWS
    tile = _tile(n, DISPATCH_TILE, SUBLANES)
    smem = pltpu.SMEM
    tiled = lambda: pl.BlockSpec((1, 1, TOP_K * tile), lambda i, ps, pe, nu: (i, 0, 0), memory_space=smem)
    grid_spec = pltpu.PrefetchScalarGridSpec(
        num_scalar_prefetch=3, grid=(n // tile,),
        in_specs=[tiled(), pl.BlockSpec((tile, d), lambda i, ps, pe, nu: (i, 0)), pl.BlockSpec(memory_space=pl.ANY)],
        out_specs=pl.BlockSpec(memory_space=pl.ANY),
        scratch_shapes=[pltpu.SemaphoreType.DMA((1,)), pltpu.SemaphoreType.DMA((1,))])
    return pl.pallas_call(
        functools.partial(_dispatch_body, tile=tile, rows=rows, nb=nb), grid_spec=grid_spec,
        out_shape=_sds((nb * rows, d), F32), compiler_params=_cp(("arbitrary",)), name="moe_dispatch",
    )(pad_start, pad_end, n_used, _tile_major(dest, tile), x, jnp.zeros((rows, d), F32))


def _expert_body(be_ref, nu_ref, x_ref, wg_ref, wu_ref, wd_ref, y_ref, wg_b, wu_b, wd_b):
    i = pl.program_id(0)
    n_used = nu_ref[0]

    @pl.when(i < n_used)
    def _():
        @pl.when(jnp.logical_or(i == 0, be_ref[i] != be_ref[jnp.maximum(i - 1, 0)]))
        def _():
            wg_b[...] = wg_ref[...].astype(BF16)
            wu_b[...] = wu_ref[...].astype(BF16)
            wd_b[...] = wd_ref[...].astype(BF16)

        y_ref[...] = _ffn(x_ref[...].astype(BF16), wg_b[...], wu_b[...], wd_b[...])

    @pl.when(i >= n_used)
    def _():
        y_ref[...] = jnp.zeros(y_ref.shape, F32)


def _expert_ffn(xs, block_e, n_used, wg, wu, wd):
    d = xs.shape[1]
    nb = block_e.shape[0]
    rows = MOE_ROWS
    f = wg.shape[2]
    grid_spec = pltpu.PrefetchScalarGridSpec(
        num_scalar_prefetch=2, grid=(nb,),
        in_specs=[pl.BlockSpec((rows, d), lambda i, be, nu: (i, 0)),
                  pl.BlockSpec((None, d, f), lambda i, be, nu: (be[i], 0, 0)),
                  pl.BlockSpec((None, d, f), lambda i, be, nu: (be[i], 0, 0)),
                  pl.BlockSpec((None, f, d), lambda i, be, nu: (be[i], 0, 0))],
        out_specs=pl.BlockSpec((rows, d), lambda i, be, nu: (i, 0)),
        scratch_shapes=[pltpu.VMEM((d, f), BF16), pltpu.VMEM((d, f), BF16), pltpu.VMEM((f, d), BF16)])
    return pl.pallas_call(
        _expert_body, grid_spec=grid_spec, out_shape=_sds((nb * rows, d), F32),
        compiler_params=_cp(("arbitrary",)), name="expert_ffn",
    )(block_e, n_used, xs, wg, wu, wd)


def _combine_body(d0_ref, d1_ref, y_hbm, gate_ref, sh_ref, x_ref, g_ref, b_ref, *rest, alpha, split_tiles):
    out_refs, (ybuf, sem) = rest[:-2], rest[-2:]
    i = pl.program_id(0)
    nt = pl.num_programs(0)
    slot = i % 2
    nrow = ybuf.shape[1]
    tc = x_ref.shape[0]

    def issue(dest_ref, dst_slot):
        def one(j, carry):
            row = dest_ref[0, 0, j]
            pltpu.make_async_copy(y_hbm.at[pl.ds(row, 1)], ybuf.at[dst_slot, pl.ds(j, 1)], sem.at[dst_slot]).start()
            return carry
        lax.fori_loop(0, nrow, one, 0, unroll=8)

    @pl.when(i == 0)
    def _():
        issue(d0_ref, 0)

    @pl.when(i + 1 < nt)
    def _():
        issue(d1_ref, 1 - slot)

    pltpu.make_async_copy(y_hbm.at[pl.ds(0, nrow)], ybuf.at[slot], sem.at[slot]).wait()
    gate = gate_ref[...]
    routed = ybuf[slot, 0:tc, :] * gate[:, 0:1]
    for k in range(1, TOP_K):
        routed = routed + ybuf[slot, k * tc:(k + 1) * tc, :] * gate[:, k:k + 1]
    y = _layer_norm_rows(alpha * x_ref[...] + (routed + sh_ref[...]), g_ref[...], b_ref[...])
    if split_tiles is None:
        out_refs[0][...] = y
        out_refs[1][...] = y.astype(BF16)
    else:
        @pl.when(i < split_tiles)
        def _():
            out_refs[0][...] = y

        @pl.when(i >= split_tiles)
        def _():
            out_refs[1][...] = y


def _combine(y_sorted, dest, gate_t, shared, x, g, b, alpha, split_rows):
    n, d = x.shape
    tc = _tile(n if split_rows is None else math.gcd(split_rows, n - split_rows), COMBINE_TILE, SUBLANES)
    nt = n // tc
    smem = pltpu.SMEM
    rows = lambda: pl.BlockSpec((tc, d), lambda i: (i, 0))
    if split_rows is None:
        st = None
        out_specs = [rows(), rows()]
        out_shape = [_sds((n, d), F32), _sds((n, d), BF16)]
    else:
        st = split_rows // tc
        out_specs = [pl.BlockSpec((tc, d), lambda i: (jnp.minimum(i, st - 1), 0)),
                     pl.BlockSpec((tc, d), lambda i: (jnp.maximum(i - st, 0), 0))]
        out_shape = [_sds((split_rows, d), F32), _sds((n - split_rows, d), F32)]
    dest_t = _tile_major(dest, tc)
    return pl.pallas_call(
        functools.partial(_combine_body, alpha=alpha, split_tiles=st), grid=(nt,),
        in_specs=[pl.BlockSpec((1, 1, TOP_K * tc), lambda i: (i, 0, 0), memory_space=smem),
                  pl.BlockSpec((1, 1, TOP_K * tc), lambda i: (jnp.minimum(i + 1, nt - 1), 0, 0), memory_space=smem),
                  pl.BlockSpec(memory_space=pl.ANY), pl.BlockSpec((tc, TOP_K), lambda i: (i, 0)), rows(), rows(),
                  _full((1, d)), _full((1, d))],
        out_specs=out_specs, out_shape=out_shape,
        scratch_shapes=[pltpu.VMEM((2, TOP_K * tc, d), F32), pltpu.SemaphoreType.DMA((2,))],
        compiler_params=_cp(("arbitrary",)), name="moe_combine",
    )(dest_t, dest_t, y_sorted, gate_t, shared, x, g.reshape(1, d), b.reshape(1, d))


def _moe_ln(x, xb, p, alpha, split_rows):
    n, d = x.shape
    idx, gate, rank, cnt = _router(x, p['router_w'], p['router_bias'])
    counts = cnt[:, 0].astype(I32)
    rows = MOE_ROWS
    padded = (counts + rows - 1) // rows * rows
    pad_end = jnp.cumsum(padded)
    pad_start = pad_end - padded
    nb = (n * TOP_K) // rows + N_EXPERTS
    n_used = (pad_end[-1] // rows).astype(I32).reshape(1)
    block_start = jnp.arange(nb, dtype=I32) * rows
    block_e = jnp.minimum(jnp.sum((pad_end[None, :] <= block_start[:, None]).astype(I32), axis=1), N_EXPERTS - 1)
    dest = _dest_rows(idx, rank, pad_start)
    xs = _dispatch(x, dest, pad_start, pad_end, n_used, nb)
    y_sorted = _expert_ffn(xs, block_e, n_used, p['exp_w_gate'], p['exp_w_up'], p['exp_w_down'])
    shared = _shared_ffn(xb, p['sh_w_gate'].astype(BF16), p['sh_w_up'].astype(BF16), p['sh_w_down'].astype(BF16))
    return _combine(y_sorted, dest, gate.T, shared, x, p['ln2_g'], p['ln2_b'], alpha, split_rows)


def _split_w_in(w_in):
    a0, b0, c0, d0 = 0, COLS_A, COLS_A + COLS_B, COLS_A + COLS_B + COLS_C
    w_dt = jnp.pad(w_in[:, c0 + SSD_INNER + SSD_CONV_DIM:d0], ((0, 0), (0, LANES - SSD_HEADS)))
    return w_in[:, a0:b0], w_in[:, b0:c0], w_in[:, c0:c0 + SSD_INNER + SSD_CONV_DIM], w_dt, w_in[:, d0:]


def _mixers(projs, st, p, bsz, seq, pos0, precise):
    proj_a, proj_b, proj_c, dt_raw, proj_d = projs
    o_a, ret_s = _retention(proj_a, st['ret'], p['ret_norm_w'], bsz, seq, pos0, precise)
    o_b, lru_h = _rg_lru(proj_b, st['lru_conv'], st['lru_h'], p, bsz, seq, pos0, precise)
    o_c, ssd_s = _ssd(proj_c, dt_raw, st['ssd_conv'], st['ssd'], p, bsz, seq, precise)
    r, lw, k, v, nkk, bb, bonus, og = _rwkv_pre(proj_d, st['rwkv_shift'], p, bsz, seq, precise)
    o, rwkv_s = _rwkv_scan(r, lw, k, v, nkk, bb, st['rwkv'], bsz, seq, precise)
    o_d = _rwkv_post(o, bonus, og, p, precise)
    last = lambda a, nrow: a.reshape(bsz, seq, -1)[:, seq - nrow:]
    new = {'ret': ret_s, 'lru_h': lru_h, 'lru_conv': last(proj_b, 3)[:, :, :LRU_WIDTH], 'ssd': ssd_s,
           'ssd_conv': last(proj_c, 3)[:, :, SSD_INNER:], 'rwkv': rwkv_s, 'rwkv_shift': last(proj_d, 1)}
    return [o_a, o_b, o_c, o_d], new


def _layer(x, xb, groups, states, p, alpha, last):
    (bp, lp, pos_p), (bs, ls, pos_s) = groups
    n_p = bp * lp
    w_parts = _split_w_in(p['w_in'])
    x_s = x[n_p:]
    projs_p = [_matmul(xb, w.astype(BF16), n_p) for w in w_parts]
    projs_s = [_matmul_f32(x_s, w) for w in w_parts]
    mix_p, new_p = _mixers(projs_p, states[0], p, bp, lp, pos_p, False)
    mix_s, new_s = _mixers(projs_s, states[1], p, bs, ls, pos_s, True)
    x1, x1b = _out_ln(mix_p, mix_s, p['w_out'], x, p['ln1_g'], p['ln1_b'], alpha)
    out_a, out_b = _moe_ln(x1, x1b, p, alpha, n_p if last else None)
    return out_a, out_b, (new_p, new_s)


_PARAM_NAMES = ('w_in', 'w_out', 'ret_norm_w', 'lru_conv_w', 'lru_conv_b', 'lru_wa', 'lru_ba', 'lru_wx', 'lru_bx',
                'lru_lambda', 'lru_norm_w', 'ssd_conv_w', 'ssd_conv_b', 'ssd_dt_bias', 'ssd_A_log', 'ssd_D',
                'ssd_norm_w', 'rwkv_mu', 'rwkv_w0', 'rwkv_w2', 'rwkv_a0', 'rwkv_a2', 'rwkv_g2', 'rwkv_k_k',
                'rwkv_k_a', 'rwkv_r_k', 'rwkv_ln_w', 'rwkv_ln_b', 'ln1_g', 'ln1_b', 'router_w', 'router_bias',
                'exp_w_gate', 'exp_w_up', 'exp_w_down', 'sh_w_gate', 'sh_w_up', 'sh_w_down', 'ln2_g', 'ln2_b')
_STATE_NAMES = ('ret', 'lru_h', 'lru_conv', 'ssd', 'ssd_conv', 'rwkv', 'rwkv_shift')


def _forward(x_prompt, x_sample, states_s, params):
    depth = params['w_in'].shape[0]
    bp, lp, d = x_prompt.shape
    bs, ls, _ = x_sample.shape
    alpha = (2 * depth) ** 0.25
    groups = ((bp, lp, 0), (bs, ls, PAST_LEN))
    x = jnp.concatenate([x_prompt.reshape(bp * lp, d), x_sample.reshape(bs * ls, d)], axis=0)
    xb = x.astype(BF16)
    out_p = {n: [] for n in _STATE_NAMES}
    out_s = {n: [] for n in _STATE_NAMES}
    for l in range(depth):
        p = {n: params[n][l] for n in _PARAM_NAMES}
        st_s = {n: states_s[n][l] for n in _STATE_NAMES}
        st_p = {n: jnp.zeros((bp,) + st_s[n].shape[1:], F32) for n in _STATE_NAMES}
        out_a, out_b, (new_p, new_s) = _layer(x, xb, groups, (st_p, st_s), p, alpha, l == depth - 1)
        x, xb = out_a, out_b
        for n in _STATE_NAMES:
            out_p[n].append(new_p[n])
            out_s[n].append(new_s[n])
    stk = lambda lst: jnp.stack(lst, axis=0)
    outs = [out_a.reshape(bp, lp, d), out_b.reshape(bs, ls, d)]
    for n in _STATE_NAMES:
        outs += [stk(out_p[n]), stk(out_s[n])]
    return tuple(outs)


def kernel(x_prompt, x_sample, state_ret, state_lru, cache_lru_conv, state_ssm, cache_ssm_conv, state_rwkv, cache_rwkv_shift, w_in, w_out, ret_norm_w, lru_conv_w, lru_conv_b, lru_wa, lru_ba, lru_wx, lru_bx, lru_lambda, lru_norm_w, ssd_conv_w, ssd_conv_b, ssd_dt_bias, ssd_A_log, ssd_D, ssd_norm_w, rwkv_mu, rwkv_w0, rwkv_w2, rwkv_a0, rwkv_a2, rwkv_g2, rwkv_k_k, rwkv_k_a, rwkv_r_k, rwkv_ln_w, rwkv_ln_b, ln1_g, ln1_b, router_w, router_bias, exp_w_gate, exp_w_up, exp_w_down, sh_w_gate, sh_w_up, sh_w_down, ln2_g, ln2_b):
    states_s = {'ret': state_ret, 'lru_h': state_lru, 'lru_conv': cache_lru_conv, 'ssd': state_ssm,
                'ssd_conv': cache_ssm_conv, 'rwkv': state_rwkv, 'rwkv_shift': cache_rwkv_shift}
    params = dict(zip(_PARAM_NAMES, (
        w_in, w_out, ret_norm_w, lru_conv_w, lru_conv_b, lru_wa, lru_ba, lru_wx, lru_bx, lru_lambda, lru_norm_w,
        ssd_conv_w, ssd_conv_b, ssd_dt_bias, ssd_A_log, ssd_D, ssd_norm_w, rwkv_mu, rwkv_w0, rwkv_w2, rwkv_a0,
        rwkv_a2, rwkv_g2, rwkv_k_k, rwkv_k_a, rwkv_r_k, rwkv_ln_w, rwkv_ln_b, ln1_g, ln1_b, router_w, router_bias,
        exp_w_gate, exp_w_up, exp_w_down, sh_w_gate, sh_w_up, sh_w_down, ln2_g, ln2_b)))
    return _forward(x_prompt, x_sample, states_s, params)
```

```python
import functools
import math

import numpy as np
import jax
import jax.numpy as jnp
from jax import lax
from jax.experimental import pallas as pl
from jax.experimental.pallas import tpu as pltpu

F32 = jnp.float32
BF16 = jnp.bfloat16
I32 = jnp.int32

D_MODEL = 2048
PAST_LEN = 1024
RET_HEADS, RET_DK, RET_DV = 4, 64, 128
LRU_WIDTH, LRU_BLOCKS, LRU_C = 512, 4, 8.0
SSD_INNER, SSD_HEADDIM, SSD_HEADS, SSD_GROUPS, SSD_STATE = 512, 64, 8, 2, 128
RWKV_WIDTH, RWKV_HEADDIM, RWKV_HEADS = 512, 64, 8
RWKV_LN_EPS = 64e-5
N_EXPERTS, TOP_K, N_EXPERT_GROUPS, TOPK_GROUPS = 64, 8, 8, 4
D_EXPERT = 512
ROUTED_SCALE = 2.5
LN_EPS = 1e-5
RMS_EPS = 1e-6
A_QK = RET_HEADS * RET_DK
A_V = RET_HEADS * RET_DV
COLS_A = 2 * A_QK + 2 * A_V
COLS_B = 2 * LRU_WIDTH
SSD_CONV_DIM = SSD_INNER + 2 * SSD_GROUPS * SSD_STATE
COLS_C = SSD_INNER + SSD_CONV_DIM + SSD_HEADS
COLS_D = 3 * RWKV_WIDTH + 64 + 64 + 128

VMEM_LIMIT_BYTES = 56 * 1024 * 1024
LANES = 128
SUBLANES = 8

TOK_TILE = 512
LN_TILE = 256
MOE_ROWS = 512
COMBINE_TILE = 64
DEST_TILE = 4096
ROUTER_TILE = 256
RET_CHUNK = 256
LRU_CHUNK = 256
SSD_CHUNK = 128
RWKV_CHUNK = 64
RWKV_PRE_TILE = 256

HIGHEST = lax.Precision.HIGHEST


def _cp(sem, vmem=VMEM_LIMIT_BYTES):
    return pltpu.CompilerParams(dimension_semantics=sem, vmem_limit_bytes=vmem)


def _sds(shape, dtype):
    return jax.ShapeDtypeStruct(shape, dtype)


def _full(shape):
    nd = len(shape)
    return pl.BlockSpec(shape, lambda *_: (0,) * nd)


def _tile(n, pref, mult=16):
    for t in range(min(pref, n) // mult * mult, 0, -mult):
        if n % t == 0:
            return t
    return n


def _sigmoid(x):
    return jax.nn.sigmoid(x)


def _silu(x):
    return x * jax.nn.sigmoid(x)


def _softplus(x):
    return jnp.maximum(x, 0.0) + jnp.log1p(jnp.exp(-jnp.abs(x)))


class _Dots:
    def __init__(self, precise):
        self.precise = precise

    def _dg(self, a, b, ca, cb):
        if self.precise:
            return lax.dot_general(a, b, (((ca,), (cb,)), ((), ())), preferred_element_type=F32, precision=HIGHEST)
        return lax.dot_general(a.astype(BF16), b.astype(BF16), (((ca,), (cb,)), ((), ())),
                               preferred_element_type=F32)

    def nn(self, a, b):
        return self._dg(a, b, 1, 0)

    def nt(self, a, b):
        return self._dg(a, b, 1, 1)

    def tn(self, a, b):
        return self._dg(a, b, 0, 0)


def _fdot(a, b):
    return jnp.dot(a, b, preferred_element_type=F32, precision=HIGHEST)


def _mm_body(x_ref, w_ref, o_ref):
    o_ref[...] = jnp.dot(x_ref[...], w_ref[...], preferred_element_type=F32)


def _matmul(x, w, rows):
    k = x.shape[1]
    n = w.shape[1]
    tm = _tile(rows, TOK_TILE)
    return pl.pallas_call(
        _mm_body, grid=(rows // tm,),
        in_specs=[pl.BlockSpec((tm, k), lambda i: (i, 0)), _full((k, n))],
        out_specs=pl.BlockSpec((tm, n), lambda i: (i, 0)),
        out_shape=_sds((rows, n), F32), compiler_params=_cp(("parallel",)), name="proj_matmul")(x, w)


def _mm_f32_body(x_ref, w_ref, o_ref):
    o_ref[...] = _fdot(x_ref[...], w_ref[...])


def _matmul_f32(x, w):
    m, k = x.shape
    n = w.shape[1]
    tn = _tile(n, 2 * LANES, LANES)
    return pl.pallas_call(
        _mm_f32_body, grid=(n // tn,),
        in_specs=[_full((m, k)), pl.BlockSpec((k, tn), lambda j: (0, j))],
        out_specs=pl.BlockSpec((m, tn), lambda j: (0, j)),
        out_shape=_sds((m, n), F32), compiler_params=_cp(("parallel",)), name="proj_matmul_f32")(x, w)


def _ret_body(q_ref, k_ref, v_ref, g_ref, cos_ref, sin_ref, dmask_ref, qdec_ref, kdec_ref, cdec_ref,
              nw_ref, s0_ref, o_ref, sout_ref, s_scr, *, precise):
    c = pl.program_id(1)
    dot = _Dots(precise)

    @pl.when(c == 0)
    def _():
        s_scr[...] = s0_ref[0]

    t = q_ref.shape[0]
    lane = lax.broadcasted_iota(I32, (t, LANES), 1)
    first = (lane % RET_DK) < (RET_DK // 2)
    cos = cos_ref[...]
    sin = sin_ref[...]

    def rot(x):
        other = jnp.where(first, pltpu.roll(x, LANES - RET_DK // 2, 1), pltpu.roll(x, RET_DK // 2, 1))
        return x * cos + other * sin

    qs = [rot(q_ref[:, p * LANES:(p + 1) * LANES]) for p in range(2)]
    ks = [rot(k_ref[:, p * LANES:(p + 1) * LANES]) * (RET_DK ** -0.5) for p in range(2)]
    kds = [ks[p] * kdec_ref[p] for p in range(2)]
    for h in range(RET_HEADS):
        lo = (h % 2) * RET_DK
        qh = qs[h // 2][:, lo:lo + RET_DK]
        kh = ks[h // 2][:, lo:lo + RET_DK]
        kdh = kds[h // 2][:, lo:lo + RET_DK]
        vh = v_ref[:, h * RET_DV:(h + 1) * RET_DV]
        s = s_scr[h]
        att = dot.nt(qh, kh) * dmask_ref[h]
        o = dot.nn(att, vh) + dot.nn(qh, s) * qdec_ref[h]
        s_scr[h] = s * cdec_ref[h] + dot.tn(kdh, vh)
        on = o * lax.rsqrt(jnp.mean(o * o, axis=-1, keepdims=True) + RMS_EPS)
        sl = slice(h * RET_DV, (h + 1) * RET_DV)
        o_ref[:, sl] = (on * nw_ref[:, sl] * _silu(g_ref[:, sl])).astype(o_ref.dtype)

    @pl.when(c == pl.num_programs(1) - 1)
    def _():
        sout_ref[0] = s_scr[...]


def _ret_tables(seq, t, pos0):
    half = RET_DK // 2
    freq = 10000.0 ** (-jnp.linspace(0.0, 1.0, half, dtype=F32))
    pos = (pos0 + jnp.arange(seq, dtype=jnp.int32)).astype(F32)
    ang = pos[:, None] * freq
    cos = jnp.cos(ang)
    sin = jnp.sin(ang)
    cos_t = jnp.tile(cos, (1, LANES // half))
    sin_t = jnp.tile(jnp.concatenate([-sin, sin], axis=1), (1, LANES // RET_DK))
    lg = jnp.log1p(-(2.0 ** (-5.0 - jnp.arange(RET_HEADS, dtype=F32))))
    idx = jnp.arange(t, dtype=F32)
    rel = idx[:, None] - idx[None, :]
    causal = rel >= 0
    dmask = jnp.where(causal[None], jnp.exp(jnp.where(causal, rel, 0.0)[None] * lg[:, None, None]), 0.0)
    q_dec = jnp.exp((idx[None, :] + 1.0) * lg[:, None])
    k_dec = jnp.exp((t - 1.0 - idx[None, :]) * lg[:, None])
    c_dec = jnp.exp(t * lg)
    qdec = jnp.broadcast_to(q_dec[:, :, None], (RET_HEADS, t, RET_DV))
    kdec = jnp.repeat(k_dec.T, RET_DK, axis=1).reshape(t, 2, LANES).transpose(1, 0, 2)
    cdec = jnp.broadcast_to(c_dec[:, None, None], (RET_HEADS, 1, RET_DV))
    return cos_t, sin_t, dmask, qdec, kdec, cdec


def _retention(proj_a, s0, norm_w, bsz, seq, pos0, precise):
    t = min(RET_CHUNK, seq)
    nc = seq // t
    cos_t, sin_t, dmask, qdec, kdec, cdec = _ret_tables(seq, t, pos0)

    def rows(w, col):
        return pl.BlockSpec((t, w), lambda b, c: (b * nc + c, col))

    o, s_out = pl.pallas_call(
        functools.partial(_ret_body, precise=precise), grid=(bsz, nc),
        in_specs=[rows(A_QK, 0), rows(A_QK, 1), rows(A_V, 1), rows(A_V, 2),
                  pl.BlockSpec((t, LANES), lambda b, c: (c, 0)),
                  pl.BlockSpec((t, LANES), lambda b, c: (c, 0)),
                  _full((RET_HEADS, t, t)), _full((RET_HEADS, t, RET_DV)), _full((2, t, LANES)),
                  _full((RET_HEADS, 1, RET_DV)), _full((1, A_V)),
                  pl.BlockSpec((1, RET_HEADS, RET_DK, RET_DV), lambda b, c: (b, 0, 0, 0))],
        out_specs=[pl.BlockSpec((t, A_V), lambda b, c: (b * nc + c, 0)),
                   pl.BlockSpec((1, RET_HEADS, RET_DK, RET_DV), lambda b, c: (b, 0, 0, 0))],
        out_shape=[_sds((bsz * seq, A_V), F32 if precise else BF16), _sds((bsz, RET_HEADS, RET_DK, RET_DV), F32)],
        scratch_shapes=[pltpu.VMEM((RET_HEADS, RET_DK, RET_DV), F32)],
        compiler_params=_cp(("parallel", "arbitrary")), name="retention",
    )(proj_a, proj_a, proj_a, proj_a, cos_t, sin_t, dmask, qdec, kdec, cdec, norm_w.reshape(1, A_V), s0)
    return o, s_out


def _conv_chunk(x, xp_scr, cw_ref, cb_ref, t):
    kw = cw_ref.shape[0]
    xp_scr[SUBLANES:SUBLANES + t, :] = x
    y = cb_ref[...] + xp_scr[SUBLANES - kw + 1:SUBLANES - kw + 1 + t, :] * cw_ref[0:1, :]
    for j in range(1, kw):
        off = SUBLANES - kw + 1 + j
        y = y + xp_scr[off:off + t, :] * cw_ref[j:j + 1, :]
    xp_scr[0:SUBLANES, :] = xp_scr[t:t + SUBLANES, :]
    return y


def _pad_conv_state(buf):
    return jnp.pad(buf, ((0, 0), (SUBLANES - buf.shape[1], 0), (0, 0)))


def _lru_body(xb_ref, gb_ref, cbuf_ref, h0_ref, cw_ref, cb_ref, wa_ref, ba_ref, wx_ref, bx_ref, lam_ref,
              nw_ref, o_ref, hout_ref, xp_scr, h_scr, *, first_pos_is_zero, precise):
    c = pl.program_id(1)
    dot = _Dots(precise)
    t = xb_ref.shape[0]

    @pl.when(c == 0)
    def _():
        xp_scr[0:SUBLANES, :] = cbuf_ref[0]
        h_scr[...] = h0_ref[0]

    xc = _conv_chunk(xb_ref[...], xp_scr, cw_ref, cb_ref, t)
    bw = LRU_WIDTH // LRU_BLOCKS
    ra, ri = [], []
    for n in range(LRU_BLOCKS):
        xh = xc[:, n * bw:(n + 1) * bw]
        ra.append(dot.nn(xh, wa_ref[n]))
        ri.append(dot.nn(xh, wx_ref[n]))
    r = _sigmoid(jnp.concatenate(ra, axis=1) + ba_ref[...])
    i = _sigmoid(jnp.concatenate(ri, axis=1) + bx_ref[...])
    log_a = -LRU_C * r * _softplus(-lam_ref[...])
    a = jnp.exp(log_a)
    th = jnp.tanh(log_a)
    mult = jnp.sqrt(-2.0 * th / (1.0 - th))
    row = lax.broadcasted_iota(I32, (t, LRU_WIDTH), 0)
    if first_pos_is_zero:
        mult = jnp.where((row + c * t) == 0, 1.0, mult)
    b = mult * (i * xc)
    s = 1
    while s < t:
        keep = row >= s
        a_sh = jnp.where(keep, pltpu.roll(a, s, 0), 1.0)
        b_sh = jnp.where(keep, pltpu.roll(b, s, 0), 0.0)
        b = a * b_sh + b
        a = a * a_sh
        s *= 2
    h = a * h_scr[...] + b
    h_scr[...] = h[t - 1:t, :]
    gb = gb_ref[...]
    gelu = 0.5 * gb * (1.0 + jnp.tanh(math.sqrt(2.0 / math.pi) * (gb + 0.044715 * (gb * gb * gb))))
    y = h * gelu
    yn = y * lax.rsqrt(jnp.mean(y * y, axis=-1, keepdims=True) + RMS_EPS)
    o_ref[...] = (yn * nw_ref[...]).astype(o_ref.dtype)

    @pl.when(c == pl.num_programs(1) - 1)
    def _():
        hout_ref[0] = h_scr[...]


def _rg_lru(proj_b, cbuf, h0, p, bsz, seq, pos0, precise):
    t = min(LRU_CHUNK, seq)
    nc = seq // t
    w = LRU_WIDTH
    bw = w // LRU_BLOCKS
    row1 = lambda v: v.reshape(1, w)
    o, h_out = pl.pallas_call(
        functools.partial(_lru_body, first_pos_is_zero=(pos0 == 0), precise=precise), grid=(bsz, nc),
        in_specs=[pl.BlockSpec((t, w), lambda b, c: (b * nc + c, 0)),
                  pl.BlockSpec((t, w), lambda b, c: (b * nc + c, 1)),
                  pl.BlockSpec((1, SUBLANES, w), lambda b, c: (b, 0, 0)),
                  pl.BlockSpec((1, 1, w), lambda b, c: (b, 0, 0)),
                  _full((4, w)), _full((1, w)), _full((LRU_BLOCKS, bw, bw)), _full((1, w)),
                  _full((LRU_BLOCKS, bw, bw)), _full((1, w)), _full((1, w)), _full((1, w))],
        out_specs=[pl.BlockSpec((t, w), lambda b, c: (b * nc + c, 0)),
                   pl.BlockSpec((1, 1, w), lambda b, c: (b, 0, 0))],
        out_shape=[_sds((bsz * seq, w), F32 if precise else BF16), _sds((bsz, 1, w), F32)],
        scratch_shapes=[pltpu.VMEM((t + SUBLANES, w), F32), pltpu.VMEM((1, w), F32)],
        compiler_params=_cp(("parallel", "arbitrary")), name="rg_lru",
    )(proj_b, proj_b, _pad_conv_state(cbuf), h0.reshape(bsz, 1, w), p['lru_conv_w'], row1(p['lru_conv_b']),
      p['lru_wa'], row1(p['lru_ba']), p['lru_wx'], row1(p['lru_bx']),
      row1(p['lru_lambda']), row1(p['lru_norm_w']))
    return o, h_out.reshape(bsz, w)


def _ssd_body(z_ref, xs_ref, bc_ref, dt_ref, dtt_ref, cbuf_ref, s0_ref, cw_ref, cb_ref, dtb_ref, dtbc_ref,
              arow_ref, acol_ref, dexp_ref, nw_ref, tril_ref, triu_ref, o_ref, sout_ref, xp_scr, st_scr, *, precise):
    c = pl.program_id(1)
    dot = _Dots(precise)
    t = z_ref.shape[0]
    npair = SSD_HEADS // 2

    @pl.when(c == 0)
    def _():
        xp_scr[0:SUBLANES, :] = cbuf_ref[0]
        for p in range(npair):
            st_scr[p] = s0_ref[0, 2 * p:2 * p + 2].reshape(LANES, SSD_STATE).T

    xin = jnp.concatenate([xs_ref[...], bc_ref[...]], axis=1)
    xbc = _silu(_conv_chunk(xin, xp_scr, cw_ref, cb_ref, t))
    xs = xbc[:, :SSD_INNER]
    gs = SSD_GROUPS * SSD_STATE
    bm = [xbc[:, SSD_INNER + g * SSD_STATE:SSD_INNER + (g + 1) * SSD_STATE] for g in range(SSD_GROUPS)]
    cm = [xbc[:, SSD_INNER + gs + g * SSD_STATE:SSD_INNER + gs + (g + 1) * SSD_STATE] for g in range(SSD_GROUPS)]
    dt_c = _softplus(dt_ref[...] + dtb_ref[...])
    dt_r = _softplus(dtt_ref[0] + dtbc_ref[:, :t])
    cum_c = _fdot(tril_ref[...], dt_c * arow_ref[...])
    cum_r = _fdot(dt_r * acol_ref[:, :t], triu_ref[...])
    cb = [dot.nt(cm[g], bm[g]) for g in range(SSD_GROUPS)]
    rowi = lax.broadcasted_iota(I32, (t, t), 0)
    coli = lax.broadcasted_iota(I32, (t, t), 1)
    tri = rowi >= coli
    lane = lax.broadcasted_iota(I32, (t, LANES), 1)
    lo = lane < SSD_HEADDIM
    ys = []
    for p in range(npair):
        g = (2 * p) // (SSD_HEADS // SSD_GROUPS)
        x_pair = xs[:, p * LANES:(p + 1) * LANES]
        y_pair = jnp.zeros((t, LANES), F32)
        cc, cend, te = [], [], []
        for hh in range(2):
            h = 2 * p + hh
            cch = cum_c[:, h:h + 1]
            seg = cch - cum_r[h:h + 1, :]
            lmat = jnp.where(tri, jnp.exp(jnp.where(tri, seg, 0.0)), 0.0)
            m = cb[g] * lmat * dt_r[h:h + 1, :]
            xm = jnp.where(lo if hh == 0 else jnp.logical_not(lo), x_pair, 0.0)
            y_pair = y_pair + dot.nn(m, xm)
            ce = cum_c[t - 1:t, h:h + 1]
            cc.append(cch)
            cend.append(ce)
            te.append(jnp.exp(ce - cch) * dt_c[:, h:h + 1])
        st = st_scr[p]
        y_pair = y_pair + dot.nn(cm[g], st) * jnp.where(lo, jnp.exp(cc[0]), jnp.exp(cc[1]))
        xt = x_pair * jnp.where(lo, te[0], te[1])
        st_scr[p] = st * jnp.where(lo[0:1, :], jnp.exp(cend[0]), jnp.exp(cend[1])) + dot.tn(bm[g], xt)
        ys.append(y_pair)
    y = jnp.concatenate(ys, axis=1)
    z = z_ref[...]
    y = (y + xs * dexp_ref[...]) * _silu(z)
    gw = SSD_INNER // SSD_GROUPS
    outs = []
    for g in range(SSD_GROUPS):
        yg = y[:, g * gw:(g + 1) * gw]
        outs.append(yg * lax.rsqrt(jnp.mean(yg * yg, axis=-1, keepdims=True) + RMS_EPS))
    o_ref[...] = (jnp.concatenate(outs, axis=1) * nw_ref[...]).astype(o_ref.dtype)

    @pl.when(c == pl.num_programs(1) - 1)
    def _():
        for p in range(npair):
            sout_ref[0, 2 * p:2 * p + 2] = st_scr[p].T.reshape(2, SSD_HEADDIM, SSD_STATE)


def _ssd(proj_c, dt_raw, cbuf, s0, p, bsz, seq, precise):
    t = min(SSD_CHUNK, seq)
    nc = seq // t
    w = SSD_INNER
    cw = SSD_CONV_DIM
    dtt = jnp.swapaxes(dt_raw[:, :SSD_HEADS].reshape(bsz, seq, SSD_HEADS), 1, 2)
    dtb = p['ssd_dt_bias'].astype(F32)
    a = -jnp.exp(p['ssd_A_log'].astype(F32))
    pad8 = lambda v: jnp.pad(v, (0, LANES - SSD_HEADS)).reshape(1, LANES)
    col8 = lambda v: jnp.broadcast_to(v[:, None], (SSD_HEADS, LANES))
    dexp = jnp.repeat(p['ssd_D'].astype(F32), SSD_HEADDIM).reshape(1, w)
    tril = jnp.tril(jnp.ones((t, t), F32))
    o, s_out = pl.pallas_call(
        functools.partial(_ssd_body, precise=precise), grid=(bsz, nc),
        in_specs=[pl.BlockSpec((t, w), lambda b, c: (b * nc + c, 0)),
                  pl.BlockSpec((t, w), lambda b, c: (b * nc + c, 1)),
                  pl.BlockSpec((t, w), lambda b, c: (b * nc + c, 2)),
                  pl.BlockSpec((t, LANES), lambda b, c: (b * nc + c, 0)),
                  pl.BlockSpec((1, SSD_HEADS, t), lambda b, c: (b, 0, c)),
                  pl.BlockSpec((1, SUBLANES, cw), lambda b, c: (b, 0, 0)),
                  pl.BlockSpec((1, SSD_HEADS, SSD_HEADDIM, SSD_STATE), lambda b, c: (b, 0, 0, 0)),
                  _full((4, cw)), _full((1, cw)), _full((1, LANES)), _full((SSD_HEADS, LANES)),
                  _full((1, LANES)), _full((SSD_HEADS, LANES)), _full((1, w)), _full((1, w)),
                  _full((t, t)), _full((t, t))],
        out_specs=[pl.BlockSpec((t, w), lambda b, c: (b * nc + c, 0)),
                   pl.BlockSpec((1, SSD_HEADS, SSD_HEADDIM, SSD_STATE), lambda b, c: (b, 0, 0, 0))],
        out_shape=[_sds((bsz * seq, w), F32 if precise else BF16), _sds((bsz, SSD_HEADS, SSD_HEADDIM, SSD_STATE), F32)],
        scratch_shapes=[pltpu.VMEM((t + SUBLANES, cw), F32), pltpu.VMEM((SSD_HEADS // 2, SSD_STATE, LANES), F32)],
        compiler_params=_cp(("parallel", "arbitrary")), name="ssd",
    )(proj_c, proj_c, proj_c, dt_raw, dtt, _pad_conv_state(cbuf), s0, p['ssd_conv_w'],
      p['ssd_conv_b'].reshape(1, cw), pad8(dtb), col8(dtb), pad8(a), col8(a), dexp,
      p['ssd_norm_w'].reshape(1, w), tril, tril.T)
    return o, s_out


def _rwkv_pre_body(pd_ref, prev_ref, shift_ref, mu_ref, wlora_ref, g2_ref, w0_ref, a0_ref, kk_ref, ka_ref,
                   rk_ref, ones_ref, r_ref, lw_ref, k_ref, v_ref, nkk_ref, bb_ref, bonus_ref, og_ref, *, precise):
    c = pl.program_id(1)
    dot = _Dots(precise)
    t = pd_ref.shape[0]
    w = RWKV_WIDTH
    pd = pd_ref[...]
    prev_row = jnp.where(c == 0, shift_ref[0], prev_ref[SUBLANES - 1:SUBLANES, :])
    row = lax.broadcasted_iota(I32, pd.shape, 0)
    prev = jnp.where(row == 0, prev_row, pltpu.roll(pd, 1, 0))
    mixed = pd + (prev - pd) * mu_ref[...]
    r = mixed[:, 0:w]
    kd = mixed[:, w:2 * w]
    vd = mixed[:, 2 * w:3 * w]
    lora_in = mixed[:, 3 * w:3 * w + LANES]
    lane = lax.broadcasted_iota(I32, lora_in.shape, 1)
    lora_in = jnp.where(lane < 64, jnp.tanh(lora_in), lora_in)
    lora = dot.nn(lora_in, wlora_ref[...])
    gd = mixed[:, 3 * w + LANES:3 * w + 2 * LANES]
    og_ref[...] = dot.nn(_sigmoid(gd), g2_ref[...])
    w_log = -_softplus(-(w0_ref[...] + lora[:, :w])) - 0.5
    lw_ref[...] = -jnp.exp(w_log)
    iclr = _sigmoid(a0_ref[...] + lora[:, w:])
    kk = kd * kk_ref[...]
    ss = _fdot(kk * kk, ones_ref[...])
    kk = kk / jnp.maximum(jnp.sqrt(ss), 1e-12)
    kr = kd * (1.0 + (iclr - 1.0) * ka_ref[...])
    r_ref[...] = r
    k_ref[...] = kr
    v_ref[...] = vd
    nkk_ref[...] = -kk
    bb_ref[...] = kk * iclr
    bonus_ref[...] = _fdot(r * kr * rk_ref[...], ones_ref[...]) * vd


def _head_ones():
    hid = jnp.arange(RWKV_WIDTH) // RWKV_HEADDIM
    return (hid[:, None] == hid[None, :]).astype(F32)


def _rwkv_pre(proj_d, shift, p, bsz, seq, precise):
    t = min(RWKV_PRE_TILE, seq)
    nc = seq // t
    w = RWKV_WIDTH
    tb = t // SUBLANES
    wlora = jnp.zeros((LANES, 2 * w), F32).at[:64, :w].set(p['rwkv_w2']).at[64:, w:].set(p['rwkv_a2'])
    row1 = lambda v: v.reshape(1, -1)
    outs = pl.pallas_call(
        functools.partial(_rwkv_pre_body, precise=precise), grid=(bsz, nc),
        in_specs=[pl.BlockSpec((t, COLS_D), lambda b, c: (b * nc + c, 0)),
                  pl.BlockSpec((SUBLANES, COLS_D), lambda b, c: (jnp.maximum((b * nc + c) * tb - 1, 0), 0)),
                  pl.BlockSpec((1, 1, COLS_D), lambda b, c: (b, 0, 0)),
                  _full((1, COLS_D)), _full((LANES, 2 * w)), _full((LANES, w)), _full((1, w)), _full((1, w)),
                  _full((1, w)), _full((1, w)), _full((1, w)), _full((w, w))],
        out_specs=[pl.BlockSpec((t, w), lambda b, c: (b * nc + c, 0))] * 8,
        out_shape=[_sds((bsz * seq, w), F32)] * 8,
        compiler_params=_cp(("parallel", "parallel")), name="rwkv_pre",
    )(proj_d, proj_d, shift, row1(p['rwkv_mu']), wlora, p['rwkv_g2'], row1(p['rwkv_w0']),
      row1(p['rwkv_a0']), row1(p['rwkv_k_k']), row1(p['rwkv_k_a']), row1(p['rwkv_r_k']), _head_ones())
    return outs


def _rwkv_scan_body(r_ref, lw_ref, k_ref, v_ref, a_ref, b_ref, s0_ref, tril_ref, o_ref, sout_ref, s_scr, *, precise):
    c = pl.program_id(1)
    dot = _Dots(precise)
    hd = RWKV_HEADDIM
    gw = 4 * hd

    @pl.when(c == 0)
    def _():
        s_scr[...] = s0_ref[...]

    row = lax.broadcasted_iota(I32, (hd, gw), 0)
    lane = lax.broadcasted_iota(I32, (hd, gw), 1)
    col = jnp.bitwise_and(lane, hd - 1)
    strict = col < row
    incl = col <= row
    eye = jnp.where(col == row, 1.0, 0.0)
    lane_blk = jnp.right_shift(lane, 6)
    bd_mask = (jnp.right_shift(lax.broadcasted_iota(I32, (gw, gw), 0), 6)
               == jnp.right_shift(lax.broadcasted_iota(I32, (gw, gw), 1), 6))

    def bd(x):
        xo = x if precise else x.astype(BF16)
        return jnp.where(bd_mask, jnp.concatenate([xo] * 4, axis=0), jnp.zeros((), xo.dtype))

    tril = tril_ref[...]
    for bb in range(2):
        for g in range(RWKV_HEADS // 4):
            sl = slice(g * gw, (g + 1) * gw)
            r, lw, k, v = r_ref[bb, :, sl], lw_ref[bb, :, sl], k_ref[bb, :, sl], v_ref[bb, :, sl]
            a, b = a_ref[bb, :, sl], b_ref[bb, :, sl]
            cum = _fdot(tril, lw)
            at = a * jnp.exp(cum - lw)
            rt = r * jnp.exp(cum)
            inv = jnp.exp(-cum)
            bt = b * inv
            kt = k * inv
            to_end = jnp.exp(cum[hd - 1:hd, :] - cum)
            gram = dot.nt(jnp.concatenate([at, rt], axis=0), jnp.concatenate([bd(bt), bd(kt)], axis=0))
            l_ab = jnp.where(strict, gram[:hd, :gw], 0.0)
            l_ak = jnp.where(strict, gram[:hd, gw:], 0.0)
            m_rb = jnp.where(incl, gram[hd:, :gw], 0.0)
            m_rk = jnp.where(incl, gram[hd:, gw:], 0.0)
            t_inv = eye + l_ab
            lp = dot.nn(l_ab, bd(l_ab))
            for level in range(5):
                both = dot.nn(jnp.concatenate([t_inv, lp], axis=0), bd(lp))
                t_inv = t_inv + both[:hd]
                lp = both[hd:]
            bd_v = bd(v)
            z = dot.nn(l_ak, bd_v)
            gu = dot.nn(t_inv, jnp.concatenate([bd(at), bd(z)], axis=1))
            g_mat, u_loc = gu[:, :gw], gu[:, gw:]
            qo = dot.nn(m_rb, jnp.concatenate([bd(g_mat), bd(u_loc)], axis=1))
            q_hat = rt + qo[:, :gw]
            o_loc = qo[:, gw:] + dot.nn(m_rk, bd_v)
            s = s_scr[bb, :, sl]
            su = dot.nt(jnp.concatenate([q_hat, g_mat], axis=0), bd(s))
            o_ref[bb, :, sl] = su[:hd] + o_loc
            u = su[hd:] + u_loc
            full = dot.tn(jnp.concatenate([u, v], axis=0), jnp.concatenate([b * to_end, k * to_end], axis=0))
            s_new = s * jnp.exp(cum[hd - 1:hd, :])
            for h in range(4):
                s_new = s_new + jnp.where(lane_blk == h, full[h * hd:(h + 1) * hd, :], 0.0)
            s_scr[bb, :, sl] = s_new

    @pl.when(c == pl.num_programs(1) - 1)
    def _():
        sout_ref[...] = s_scr[...]


def _rwkv_scan(r, lw, k, v, nkk, bb, s0, bsz, seq, precise):
    wd = RWKV_WIDTH
    hd = RWKV_HEADDIM
    t = RWKV_CHUNK
    pad = (-seq) % t
    nc = (seq + pad) // t
    r3 = lambda x: jnp.pad(x.reshape(bsz, seq, wd), ((0, 0), (0, pad), (0, 0)))
    s0s = s0.transpose(0, 2, 1, 3).reshape(bsz, hd, wd)
    spec = pl.BlockSpec((2, t, wd), lambda b, c: (b, c, 0))
    sspec = pl.BlockSpec((2, hd, wd), lambda b, c: (b, 0, 0))
    o, s_out = pl.pallas_call(
        functools.partial(_rwkv_scan_body, precise=precise), grid=(bsz // 2, nc),
        in_specs=[spec] * 6 + [sspec, _full((t, t))], out_specs=[spec, sspec],
        out_shape=[_sds((bsz, seq + pad, wd), F32), _sds((bsz, hd, wd), F32)],
        scratch_shapes=[pltpu.VMEM((2, hd, wd), F32)],
        compiler_params=_cp(("parallel", "arbitrary")), name="rwkv_scan",
    )(r3(r), r3(lw), r3(k), r3(v), r3(nkk), r3(bb), s0s, jnp.tril(jnp.ones((t, t), F32)))
    s_fin = s_out.reshape(bsz, hd, RWKV_HEADS, hd).transpose(0, 2, 1, 3)
    return o[:, :seq].reshape(bsz * seq, wd), s_fin


def _rwkv_post_body(o_ref, bonus_ref, og_ref, lw_ref, lb_ref, ones_ref, out_ref):
    o = o_ref[...]
    inv = 1.0 / RWKV_HEADDIM
    mean = _fdot(o, ones_ref[...]) * inv
    xc = o - mean
    var = _fdot(xc * xc, ones_ref[...]) * inv
    y = xc * lax.rsqrt(var + RWKV_LN_EPS) * lw_ref[...] + lb_ref[...]
    out_ref[...] = ((y + bonus_ref[...]) * og_ref[...]).astype(out_ref.dtype)


def _rwkv_post(o, bonus, og, p, precise):
    n = o.shape[0]
    t = _tile(n, RWKV_PRE_TILE)
    w = RWKV_WIDTH
    spec = pl.BlockSpec((t, w), lambda i: (i, 0))
    return pl.pallas_call(
        _rwkv_post_body, grid=(n // t,),
        in_specs=[spec, spec, spec, _full((1, w)), _full((1, w)), _full((w, w))],
        out_specs=spec, out_shape=_sds((n, w), F32 if precise else BF16),
        compiler_params=_cp(("parallel",)), name="rwkv_post",
    )(o, bonus, og, p['rwkv_ln_w'].reshape(1, w), p['rwkv_ln_b'].reshape(1, w), _head_ones())


def _layer_norm_rows(z, g, b):
    zc = z - jnp.mean(z, axis=-1, keepdims=True)
    return zc * lax.rsqrt(jnp.mean(zc * zc, axis=-1, keepdims=True) + LN_EPS) * g + b


def _out_ln_body(*refs, alpha, prompt_tiles, parts):
    mixp_refs, mixs_refs = refs[:parts], refs[parts:2 * parts]
    w_hbm, x_ref, g_ref, b_ref, y_ref, yb_ref, wf_scr, wb_scr, sem = refs[2 * parts:]
    i = pl.program_id(0)
    width = mixp_refs[0].shape[1]

    @pl.when(i == 0)
    def _():
        copy = pltpu.make_async_copy(w_hbm, wf_scr, sem.at[0])
        copy.start()
        copy.wait()
        wb_scr[...] = wf_scr[...].astype(BF16)

    def finish(mix):
        y = _layer_norm_rows(alpha * x_ref[...] + mix, g_ref[...], b_ref[...])
        y_ref[...] = y
        yb_ref[...] = y.astype(BF16)

    def project(mix_refs, w_scr, dot):
        acc = dot(mix_refs[0][...], w_scr[0:width, :])
        for j in range(1, parts):
            acc = acc + dot(mix_refs[j][...], w_scr[j * width:(j + 1) * width, :])
        return acc

    @pl.when(i < prompt_tiles)
    def _():
        finish(project(mixp_refs, wb_scr, lambda a, b: jnp.dot(a, b, preferred_element_type=F32)))

    @pl.when(i >= prompt_tiles)
    def _():
        finish(project(mixs_refs, wf_scr, _fdot))


def _out_ln(mix_p, mix_s, w_out, x, g, b, alpha):
    n, d = x.shape
    parts = len(mix_p)
    n_p, n_s = mix_p[0].shape[0], mix_s[0].shape[0]
    width = mix_p[0].shape[1]
    tm = _tile(math.gcd(n_p, n_s), LN_TILE)
    pt = n_p // tm
    rows = lambda: pl.BlockSpec((tm, d), lambda i: (i, 0))
    spec_p = pl.BlockSpec((tm, width), lambda i: (jnp.minimum(i, pt - 1), 0))
    spec_s = pl.BlockSpec((tm, width), lambda i: (jnp.maximum(i - pt, 0), 0))
    return pl.pallas_call(
        functools.partial(_out_ln_body, alpha=alpha, prompt_tiles=pt, parts=parts), grid=(n // tm,),
        in_specs=[spec_p] * parts + [spec_s] * parts + [pl.BlockSpec(memory_space=pl.ANY), rows(), _full((1, d)),
                                                      _full((1, d))],
        out_specs=[rows(), rows()], out_shape=[_sds((n, d), F32), _sds((n, d), BF16)],
        scratch_shapes=[pltpu.VMEM((d, d), F32), pltpu.VMEM((d, d), BF16), pltpu.SemaphoreType.DMA((1,))],
        compiler_params=_cp(("arbitrary",)), name="out_proj_ln",
    )(*mix_p, *mix_s, w_out, x, g.reshape(1, d), b.reshape(1, d))


def _router_body(x_ref, wt_ref, bias_ref, tri_ref, idx_ref, gate_ref, rank_ref, cnt_ref, cnt_scr):
    i = pl.program_id(0)
    tm = x_ref.shape[0]
    ne, ng = N_EXPERTS, N_EXPERT_GROUPS
    per = ne // ng

    @pl.when(i == 0)
    def _():
        cnt_scr[...] = jnp.zeros(cnt_scr.shape, F32)

    logits = lax.dot_general(wt_ref[...], x_ref[...], (((1,), (1,)), ((), ())), preferred_element_type=F32,
                             precision=HIGHEST)
    scores = _sigmoid(logits)
    biased = scores + bias_ref[...]
    neg = -jnp.inf
    b3 = biased.reshape(ng, per, tm)
    e3 = lax.broadcasted_iota(I32, (ng, per, tm), 1)
    m1 = jnp.max(b3, axis=1, keepdims=True)
    i1 = jnp.min(jnp.where(b3 == m1, e3, per), axis=1, keepdims=True)
    m2 = jnp.max(jnp.where(e3 == i1, neg, b3), axis=1, keepdims=True)
    grp = (m1 + m2).reshape(ng, tm)
    gi = lax.broadcasted_iota(I32, (ng, tm), 0)
    keep = jnp.zeros((ng, tm), jnp.bool_)
    for _ in range(TOPK_GROUPS):
        m = jnp.max(grp, axis=0, keepdims=True)
        first = jnp.min(jnp.where(grp == m, gi, ng), axis=0, keepdims=True)
        sel = gi == first
        keep = jnp.logical_or(keep, sel)
        grp = jnp.where(sel, neg, grp)
    keep_e = jnp.broadcast_to(keep.astype(F32).reshape(ng, 1, tm), (ng, per, tm)).reshape(ne, tm) > 0.5
    masked = jnp.where(keep_e, biased, neg)
    ei = lax.broadcasted_iota(I32, (ne, tm), 0)
    sels, idxs, gates = [], [], []
    chosen = jnp.zeros((ne, tm), F32)
    for _ in range(TOP_K):
        m = jnp.max(masked, axis=0, keepdims=True)
        first = jnp.min(jnp.where(masked == m, ei, ne), axis=0, keepdims=True)
        sel = ei == first
        idxs.append(first)
        gates.append(jnp.sum(jnp.where(sel, scores, 0.0), axis=0, keepdims=True))
        sels.append(sel)
        chosen = chosen + sel.astype(F32)
        masked = jnp.where(sel, neg, masked)
    gsum = gates[0]
    for k in range(1, TOP_K):
        gsum = gsum + gates[k]
    before = jnp.dot(chosen.astype(BF16), tri_ref[...], preferred_element_type=F32) + cnt_scr[:, 0:tm]
    for k in range(TOP_K):
        idx_ref[k:k + 1, :] = idxs[k]
        gate_ref[k:k + 1, :] = gates[k] / gsum * ROUTED_SCALE
        rank_ref[k:k + 1, :] = jnp.sum(jnp.where(sels[k], before, 0.0), axis=0, keepdims=True).astype(I32)
    cnt_scr[...] = cnt_scr[...] + jnp.sum(chosen, axis=1, keepdims=True)
    cnt_ref[...] = cnt_scr[...]


def _router(x, router_w, router_bias):
    n, d = x.shape
    tm = _tile(n, ROUTER_TILE, LANES)
    tri = (jnp.arange(tm)[:, None] < jnp.arange(tm)[None, :]).astype(BF16)
    bias = jnp.broadcast_to(router_bias.astype(F32)[:, None], (N_EXPERTS, tm))
    tok = lambda: pl.BlockSpec((TOP_K, tm), lambda i: (0, i))
    return pl.pallas_call(
        _router_body, grid=(n // tm,),
        in_specs=[pl.BlockSpec((tm, d), lambda i: (i, 0)), _full((N_EXPERTS, d)), _full((N_EXPERTS, tm)),
                  _full((tm, tm))],
        out_specs=[tok(), tok(), tok(), _full((N_EXPERTS, tm))],
        out_shape=[_sds((TOP_K, n), I32), _sds((TOP_K, n), F32), _sds((TOP_K, n), I32), _sds((N_EXPERTS, tm), F32)],
        scratch_shapes=[pltpu.VMEM((N_EXPERTS, tm), F32)],
        compiler_params=_cp(("arbitrary",)), name="router",
    )(x, router_w.T, bias, tri)


def _ffn(xb, wg, wu, wd):
    h = _silu(jnp.dot(xb, wg, preferred_element_type=F32)) * jnp.dot(xb, wu, preferred_element_type=F32)
    return jnp.dot(h.astype(BF16), wd, preferred_element_type=F32)


def _shared_body(x_ref, wg_ref, wu_ref, wd_ref, o_ref):
    o_ref[...] = _ffn(x_ref[...], wg_ref[...], wu_ref[...], wd_ref[...])


def _shared_ffn(xb, wg, wu, wd):
    n, d = xb.shape
    tm = _tile(n, TOK_TILE)
    f = wg.shape[1]
    return pl.pallas_call(
        _shared_body, grid=(n // tm,),
        in_specs=[pl.BlockSpec((tm, d), lambda i: (i, 0)), _full((d, f)), _full((d, f)), _full((f, d))],
        out_specs=pl.BlockSpec((tm, d), lambda i: (i, 0)), out_shape=_sds((n, d), F32),
        compiler_params=_cp(("parallel",)), name="shared_ffn")(xb, wg, wu, wd)


def _dest_body(ps_ref, idx_ref, rank_ref, dest_ref):
    idx = idx_ref[...]
    base = lax.fori_loop(0, N_EXPERTS, lambda e, acc: jnp.where(idx == e, ps_ref[e], acc), jnp.zeros(idx.shape, I32))
    dest_ref[...] = base + rank_ref[...]


def _dest_rows(idx, rank, pad_start):
    n = idx.shape[1]
    tn = _tile(n, DEST_TILE, LANES)
    spec = lambda: pl.BlockSpec((TOP_K, tn), lambda i, ps: (0, i))
    grid_spec = pltpu.PrefetchScalarGridSpec(num_scalar_prefetch=1, grid=(n // tn,), in_specs=[spec(), spec()],
                                             out_specs=spec())
    return pl.pallas_call(_dest_body, grid_spec=grid_spec, out_shape=_sds((TOP_K, n), I32),
                          compiler_params=_cp(("parallel",)), name="moe_dest")(pad_start, idx, rank)


def _tile_major(a, tile):
    n = a.shape[1]
    return a.reshape(TOP_K, n // tile, tile).transpose(1, 0, 2).reshape(n // tile, 1, TOP_K * tile)


def _expert_body(be_ref, nu_ref, tok0_ref, tok1_ref, x_hbm, wg_ref, wu_ref, wd_ref, y_ref, xbuf, sem, wg_b, wu_b, wd_b):
    i = pl.program_id(0)
    n_used = nu_ref[0]
    slot = i % 2
    rows = xbuf.shape[1]

    def issue(tok_ref, dst_slot):
        def one(r, carry):
            src = x_hbm.at[pl.ds(tok_ref[0, 0, r], 1)]
            pltpu.make_async_copy(src, xbuf.at[dst_slot, pl.ds(r, 1)], sem.at[dst_slot]).start()
            return carry
        lax.fori_loop(0, rows, one, 0, unroll=8)

    @pl.when(jnp.logical_and(i == 0, n_used > 0))
    def _():
        issue(tok0_ref, 0)

    @pl.when(i + 1 < n_used)
    def _():
        issue(tok1_ref, 1 - slot)

    @pl.when(i < n_used)
    def _():
        @pl.when(jnp.logical_or(i == 0, be_ref[i] != be_ref[jnp.maximum(i - 1, 0)]))
        def _():
            wg_b[...] = wg_ref[...].astype(BF16)
            wu_b[...] = wu_ref[...].astype(BF16)
            wd_b[...] = wd_ref[...].astype(BF16)

        pltpu.make_async_copy(x_hbm.at[pl.ds(0, rows)], xbuf.at[slot], sem.at[slot]).wait()
        y_ref[...] = _ffn(xbuf[slot].astype(BF16), wg_b[...], wu_b[...], wd_b[...])

    @pl.when(i >= n_used)
    def _():
        y_ref[...] = jnp.zeros(y_ref.shape, F32)


def _expert_ffn(x, row_tok, block_e, n_used, wg, wu, wd):
    n, d = x.shape
    nb = block_e.shape[0]
    rows = MOE_ROWS
    f = wg.shape[2]
    tok3 = row_tok.reshape(nb, 1, rows)
    smem = pltpu.SMEM
    grid_spec = pltpu.PrefetchScalarGridSpec(
        num_scalar_prefetch=2, grid=(nb,),
        in_specs=[pl.BlockSpec((1, 1, rows), lambda i, be, nu: (i, 0, 0), memory_space=smem),
                  pl.BlockSpec((1, 1, rows), lambda i, be, nu: (jnp.minimum(i + 1, nb - 1), 0, 0), memory_space=smem),
                  pl.BlockSpec(memory_space=pl.ANY),
                  pl.BlockSpec((None, d, f), lambda i, be, nu: (be[i], 0, 0)),
                  pl.BlockSpec((None, d, f), lambda i, be, nu: (be[i], 0, 0)),
                  pl.BlockSpec((None, f, d), lambda i, be, nu: (be[i], 0, 0))],
        out_specs=pl.BlockSpec((rows, d), lambda i, be, nu: (i, 0)),
        scratch_shapes=[pltpu.VMEM((2, rows, d), F32), pltpu.SemaphoreType.DMA((2,)),
                        pltpu.VMEM((d, f), BF16), pltpu.VMEM((d, f), BF16), pltpu.VMEM((f, d), BF16)])
    return pl.pallas_call(
        _expert_body, grid_spec=grid_spec, out_shape=_sds((nb * rows, d), F32),
        compiler_params=_cp(("arbitrary",)), name="expert_ffn",
    )(block_e, n_used, tok3, tok3, x, wg, wu, wd)


def _combine_body(d0_ref, d1_ref, y_hbm, gate_ref, sh_ref, x_ref, g_ref, b_ref, *rest, alpha, split_tiles):
    out_refs, (ybuf, sem) = rest[:-2], rest[-2:]
    i = pl.program_id(0)
    nt = pl.num_programs(0)
    slot = i % 2
    nrow = ybuf.shape[1]
    tc = x_ref.shape[0]

    def issue(dest_ref, dst_slot):
        def one(j, carry):
            row = dest_ref[0, 0, j]
            pltpu.make_async_copy(y_hbm.at[pl.ds(row, 1)], ybuf.at[dst_slot, pl.ds(j, 1)], sem.at[dst_slot]).start()
            return carry
        lax.fori_loop(0, nrow, one, 0, unroll=8)

    @pl.when(i == 0)
    def _():
        issue(d0_ref, 0)

    @pl.when(i + 1 < nt)
    def _():
        issue(d1_ref, 1 - slot)

    pltpu.make_async_copy(y_hbm.at[pl.ds(0, nrow)], ybuf.at[slot], sem.at[slot]).wait()
    gate = gate_ref[...]
    routed = ybuf[slot, 0:tc, :] * gate[:, 0:1]
    for k in range(1, TOP_K):
        routed = routed + ybuf[slot, k * tc:(k + 1) * tc, :] * gate[:, k:k + 1]
    y = _layer_norm_rows(alpha * x_ref[...] + (routed + sh_ref[...]), g_ref[...], b_ref[...])
    if split_tiles is None:
        out_refs[0][...] = y
        out_refs[1][...] = y.astype(BF16)
    else:
        @pl.when(i < split_tiles)
        def _():
            out_refs[0][...] = y

        @pl.when(i >= split_tiles)
        def _():
            out_refs[1][...] = y


def _combine(y_sorted, dest, gate_t, shared, x, g, b, alpha, split_rows):
    n, d = x.shape
    tc = _tile(n if split_rows is None else math.gcd(split_rows, n - split_rows), COMBINE_TILE, SUBLANES)
    nt = n // tc
    smem = pltpu.SMEM
    rows = lambda: pl.BlockSpec((tc, d), lambda i: (i, 0))
    if split_rows is None:
        st = None
        out_specs = [rows(), rows()]
        out_shape = [_sds((n, d), F32), _sds((n, d), BF16)]
    else:
        st = split_rows // tc
        out_specs = [pl.BlockSpec((tc, d), lambda i: (jnp.minimum(i, st - 1), 0)),
                     pl.BlockSpec((tc, d), lambda i: (jnp.maximum(i - st, 0), 0))]
        out_shape = [_sds((split_rows, d), F32), _sds((n - split_rows, d), F32)]
    dest_t = _tile_major(dest, tc)
    return pl.pallas_call(
        functools.partial(_combine_body, alpha=alpha, split_tiles=st), grid=(nt,),
        in_specs=[pl.BlockSpec((1, 1, TOP_K * tc), lambda i: (i, 0, 0), memory_space=smem),
                  pl.BlockSpec((1, 1, TOP_K * tc), lambda i: (jnp.minimum(i + 1, nt - 1), 0, 0), memory_space=smem),
                  pl.BlockSpec(memory_space=pl.ANY), pl.BlockSpec((tc, TOP_K), lambda i: (i, 0)), rows(), rows(),
                  _full((1, d)), _full((1, d))],
        out_specs=out_specs, out_shape=out_shape,
        scratch_shapes=[pltpu.VMEM((2, TOP_K * tc, d), F32), pltpu.SemaphoreType.DMA((2,))],
        compiler_params=_cp(("arbitrary",)), name="moe_combine",
    )(dest_t, dest_t, y_sorted, gate_t, shared, x, g.reshape(1, d), b.reshape(1, d))


def _moe_ln(x, xb, p, alpha, split_rows):
    n, d = x.shape
    idx, gate, rank, cnt = _router(x, p['router_w'], p['router_bias'])
    counts = cnt[:, 0].astype(I32)
    rows = MOE_ROWS
    padded = (counts + rows - 1) // rows * rows
    pad_end = jnp.cumsum(padded)
    pad_start = pad_end - padded
    nb = (n * TOP_K) // rows + N_EXPERTS
    n_used = (pad_end[-1] // rows).astype(I32).reshape(1)
    block_start = jnp.arange(nb, dtype=I32) * rows
    block_e = jnp.minimum(jnp.sum((pad_end[None, :] <= block_start[:, None]).astype(I32), axis=1), N_EXPERTS - 1)
    dest = _dest_rows(idx, rank, pad_start)
    tok = jnp.broadcast_to(jnp.arange(n, dtype=I32)[None, :], (TOP_K, n))
    row_tok = jnp.zeros((nb * rows,), I32).at[dest.reshape(-1)].set(tok.reshape(-1), unique_indices=True)
    y_sorted = _expert_ffn(x, row_tok, block_e, n_used, p['exp_w_gate'], p['exp_w_up'], p['exp_w_down'])
    shared = _shared_ffn(xb, p['sh_w_gate'].astype(BF16), p['sh_w_up'].astype(BF16), p['sh_w_down'].astype(BF16))
    return _combine(y_sorted, dest, gate.T, shared, x, p['ln2_g'], p['ln2_b'], alpha, split_rows)


def _split_w_in(w_in):
    a0, b0, c0, d0 = 0, COLS_A, COLS_A + COLS_B, COLS_A + COLS_B + COLS_C
    w_dt = jnp.pad(w_in[:, c0 + SSD_INNER + SSD_CONV_DIM:d0], ((0, 0), (0, LANES - SSD_HEADS)))
    return w_in[:, a0:b0], w_in[:, b0:c0], w_in[:, c0:c0 + SSD_INNER + SSD_CONV_DIM], w_dt, w_in[:, d0:]


def _mixers(projs, st, p, bsz, seq, pos0, precise):
    proj_a, proj_b, proj_c, dt_raw, proj_d = projs
    o_a, ret_s = _retention(proj_a, st['ret'], p['ret_norm_w'], bsz, seq, pos0, precise)
    o_b, lru_h = _rg_lru(proj_b, st['lru_conv'], st['lru_h'], p, bsz, seq, pos0, precise)
    o_c, ssd_s = _ssd(proj_c, dt_raw, st['ssd_conv'], st['ssd'], p, bsz, seq, precise)
    r, lw, k, v, nkk, bb, bonus, og = _rwkv_pre(proj_d, st['rwkv_shift'], p, bsz, seq, precise)
    o, rwkv_s = _rwkv_scan(r, lw, k, v, nkk, bb, st['rwkv'], bsz, seq, precise)
    o_d = _rwkv_post(o, bonus, og, p, precise)
    last = lambda a, nrow: a.reshape(bsz, seq, -1)[:, seq - nrow:]
    new = {'ret': ret_s, 'lru_h': lru_h, 'lru_conv': last(proj_b, 3)[:, :, :LRU_WIDTH], 'ssd': ssd_s,
           'ssd_conv': last(proj_c, 3)[:, :, SSD_INNER:], 'rwkv': rwkv_s, 'rwkv_shift': last(proj_d, 1)}
    return [o_a, o_b, o_c, o_d], new


def _layer(x, xb, groups, states, p, alpha, last):
    (bp, lp, pos_p), (bs, ls, pos_s) = groups
    n_p = bp * lp
    w_parts = _split_w_in(p['w_in'])
    x_s = x[n_p:]
    projs_p = [_matmul(xb, w.astype(BF16), n_p) for w in w_parts]
    projs_s = [_matmul_f32(x_s, w) for w in w_parts]
    mix_p, new_p = _mixers(projs_p, states[0], p, bp, lp, pos_p, False)
    mix_s, new_s = _mixers(projs_s, states[1], p, bs, ls, pos_s, True)
    x1, x1b = _out_ln(mix_p, mix_s, p['w_out'], x, p['ln1_g'], p['ln1_b'], alpha)
    out_a, out_b = _moe_ln(x1, x1b, p, alpha, n_p if last else None)
    return out_a, out_b, (new_p, new_s)


_PARAM_NAMES = ('w_in', 'w_out', 'ret_norm_w', 'lru_conv_w', 'lru_conv_b', 'lru_wa', 'lru_ba', 'lru_wx', 'lru_bx',
                'lru_lambda', 'lru_norm_w', 'ssd_conv_w', 'ssd_conv_b', 'ssd_dt_bias', 'ssd_A_log', 'ssd_D',
                'ssd_norm_w', 'rwkv_mu', 'rwkv_w0', 'rwkv_w2', 'rwkv_a0', 'rwkv_a2', 'rwkv_g2', 'rwkv_k_k',
                'rwkv_k_a', 'rwkv_r_k', 'rwkv_ln_w', 'rwkv_ln_b', 'ln1_g', 'ln1_b', 'router_w', 'router_bias',
                'exp_w_gate', 'exp_w_up', 'exp_w_down', 'sh_w_gate', 'sh_w_up', 'sh_w_down', 'ln2_g', 'ln2_b')
_STATE_NAMES = ('ret', 'lru_h', 'lru_conv', 'ssd', 'ssd_conv', 'rwkv', 'rwkv_shift')


def _forward(x_prompt, x_sample, states_s, params):
    depth = params['w_in'].shape[0]
    bp, lp, d = x_prompt.shape
    bs, ls, _ = x_sample.shape
    alpha = (2 * depth) ** 0.25
    groups = ((bp, lp, 0), (bs, ls, PAST_LEN))
    x = jnp.concatenate([x_prompt.reshape(bp * lp, d), x_sample.reshape(bs * ls, d)], axis=0)
    xb = x.astype(BF16)
    out_p = {n: [] for n in _STATE_NAMES}
    out_s = {n: [] for n in _STATE_NAMES}
    for l in range(depth):
        p = {n: params[n][l] for n in _PARAM_NAMES}
        st_s = {n: states_s[n][l] for n in _STATE_NAMES}
        st_p = {n: jnp.zeros((bp,) + st_s[n].shape[1:], F32) for n in _STATE_NAMES}
        out_a, out_b, (new_p, new_s) = _layer(x, xb, groups, (st_p, st_s), p, alpha, l == depth - 1)
        x, xb = out_a, out_b
        for n in _STATE_NAMES:
            out_p[n].append(new_p[n])
            out_s[n].append(new_s[n])
    stk = lambda lst: jnp.stack(lst, axis=0)
    outs = [out_a.reshape(bp, lp, d), out_b.reshape(bs, ls, d)]
    for n in _STATE_NAMES:
        outs += [stk(out_p[n]), stk(out_s[n])]
    return tuple(outs)


def kernel(x_prompt, x_sample, state_ret, state_lru, cache_lru_conv, state_ssm, cache_ssm_conv, state_rwkv, cache_rwkv_shift, w_in, w_out, ret_norm_w, lru_conv_w, lru_conv_b, lru_wa, lru_ba, lru_wx, lru_bx, lru_lambda, lru_norm_w, ssd_conv_w, ssd_conv_b, ssd_dt_bias, ssd_A_log, ssd_D, ssd_norm_w, rwkv_mu, rwkv_w0, rwkv_w2, rwkv_a0, rwkv_a2, rwkv_g2, rwkv_k_k, rwkv_k_a, rwkv_r_k, rwkv_ln_w, rwkv_ln_b, ln1_g, ln1_b, router_w, router_bias, exp_w_gate, exp_w_up, exp_w_down, sh_w_gate, sh_w_up, sh_w_down, ln2_g, ln2_b):
    states_s = {'ret': state_ret, 'lru_h': state_lru, 'lru_conv': cache_lru_conv, 'ssd': state_ssm,
                'ssd_conv': cache_ssm_conv, 'rwkv': state_rwkv, 'rwkv_shift': cache_rwkv_shift}
    params = dict(zip(_PARAM_NAMES, (
        w_in, w_out, ret_norm_w, lru_conv_w, lru_conv_b, lru_wa, lru_ba, lru_wx, lru_bx, lru_lambda, lru_norm_w,
        ssd_conv_w, ssd_conv_b, ssd_dt_bias, ssd_A_log, ssd_D, ssd_norm_w, rwkv_mu, rwkv_w0, rwkv_w2, rwkv_a0,
        rwkv_a2, rwkv_g2, rwkv_k_k, rwkv_k_a, rwkv_r_k, rwkv_ln_w, rwkv_ln_b, ln1_g, ln1_b, router_w, router_bias,
        exp_w_gate, exp_w_up, exp_w_down, sh_w_gate, sh_w_up, sh_w_down, ln2_g, ln2_b)))
    return _forward(x_prompt, x_sample, states_s, params)
```
